```python
import math
import jax, jax.numpy as jnp
from jax import lax
import numpy as np

D_MODEL = 1024
BATCH = 32
SEQ = 2048
DEPTH = 4
DEC_BATCH = 2
DEC_SEQ = 8192
PAST_LEN = 128

N_MIXERS = 2
GROUP_CH = 16
N_GROUPS = D_MODEL // GROUP_CH
STATE = 64
CONV_WIDTH = 31
CONV_PAD = (CONV_WIDTH - 1) // 2
N_MEM = 256
N_XHEADS = 4
XHEAD_DIM = D_MODEL // N_XHEADS
D_FF = 4 * D_MODEL
N_SSM_LAYERS = (DEPTH + 1) // 2
N_CONV_LAYERS = DEPTH // 2
EPS = 1e-6
DT_MIN = 1e-3
DT_MAX = 1e-1

kernel_name = 'hybrid_s5_conformer_encoder'


def rmsnorm(x, g):
    xf = x.astype(jnp.float32)
    y = xf * lax.rsqrt(jnp.mean(xf * xf, axis=-1, keepdims=True) + EPS)
    return (y * g.astype(jnp.float32)).astype(x.dtype)


def layernorm(x, g, b):
    xf = x.astype(jnp.float32)
    mu = jnp.mean(xf, axis=-1, keepdims=True)
    xc = xf - mu
    var = jnp.mean(xc * xc, axis=-1, keepdims=True)
    y = xc * lax.rsqrt(var + EPS) * g.astype(jnp.float32) + b.astype(jnp.float32)
    return y.astype(x.dtype)


def _linear_recurrence(e_i, e_j):
    a_i, b_i = e_i
    a_j, b_j = e_j
    return a_j * a_i, a_j * b_i + b_j


def s5_scan(u, a_re, a_im, log_step, b_re, b_im, c_re, c_im, reverse):
    lam = lax.complex(a_re.astype(jnp.float32), a_im.astype(jnp.float32))
    step = jnp.exp(log_step.astype(jnp.float32))[:, None]
    lam_bar = jnp.exp(lam * step)
    b_mat = lax.complex(b_re.astype(jnp.float32), b_im.astype(jnp.float32))
    b_bar = ((lam_bar - 1.0) / lam)[..., None] * b_mat
    bu = jnp.einsum('blgc,gpc->blgp', u.astype(jnp.complex64), b_bar)
    a = jnp.broadcast_to(lam_bar, bu.shape)
    _, s = lax.associative_scan(_linear_recurrence, (a, bu), axis=1, reverse=reverse)
    c_mat = lax.complex(c_re.astype(jnp.float32), c_im.astype(jnp.float32))
    return jnp.real(jnp.einsum('blgp,gcp->blgc', s, c_mat))


def s5_mixer(u, j, p):
    bsz, seq, _ = u.shape
    uf = u.astype(jnp.float32)
    ug = uf.reshape(bsz, seq, N_GROUPS, GROUP_CH)
    y = p['ssm_d'][j].astype(jnp.float32) * uf
    for d, rev in ((0, False), (1, True)):
        y = y + s5_scan(ug, p['ssm_a_re'][j, d], p['ssm_a_im'][j, d], p['ssm_log_step'][j, d],
                        p['ssm_b_re'][j, d], p['ssm_b_im'][j, d],
                        p['ssm_c_re'][j, d], p['ssm_c_im'][j, d], rev).reshape(bsz, seq, D_MODEL)
    g = jax.nn.gelu(y).astype(u.dtype)
    h = g @ p['ssm_w_glu'][j] + p['ssm_b_glu'][j]
    return h[..., :D_MODEL] * jax.nn.sigmoid(h[..., D_MODEL:])


def conv_mixer(u, j, p):
    h = u @ p['conv_w_in'][j] + p['conv_b_in'][j]
    h = h[..., :D_MODEL] * jax.nn.sigmoid(h[..., D_MODEL:])
    w_dw = p['conv_w_dw'][j][:, None, :]
    h = lax.conv_general_dilated(h, w_dw.astype(h.dtype), window_strides=(1,),
                                 padding=((CONV_PAD, CONV_PAD),),
                                 dimension_numbers=('NWC', 'WIO', 'NWC'),
                                 feature_group_count=D_MODEL) + p['conv_b_dw'][j]
    h = layernorm(h, p['conv_ln_g'][j], p['conv_ln_b'][j])
    h = jax.nn.silu(h)
    return h @ p['conv_w_out'][j] + p['conv_b_out'][j]


def cross_attention(xn, mn, i, p):
    bsz, seq, _ = xn.shape
    q = (xn @ p['attn_w_q'][i]).reshape(bsz, seq, N_XHEADS, XHEAD_DIM)
    kv = mn @ p['attn_w_kv'][i]
    k = kv[..., :D_MODEL].reshape(bsz, -1, N_XHEADS, XHEAD_DIM)
    v = kv[..., D_MODEL:].reshape(bsz, -1, N_XHEADS, XHEAD_DIM)
    s = jnp.einsum('blhd,bmhd->bhlm', q.astype(jnp.float32), k.astype(jnp.float32)) * (XHEAD_DIM ** -0.5)
    pr = jax.nn.softmax(s, axis=-1).astype(xn.dtype)
    o = jnp.einsum('bhlm,bmhd->blhd', pr, v).reshape(bsz, seq, D_MODEL)
    return o @ p['attn_w_o'][i]


def sqrelu_mlp(xn, i, p):
    h = jnp.square(jax.nn.relu(xn @ p['mlp_w_in'][i]))
    return h @ p['mlp_w_out'][i]


def trunk(x, mem, p):
    for i in range(DEPTH):
        j = i // N_MIXERS
        xn = rmsnorm(x, p['norm_mix'][i])
        if i % N_MIXERS == 0:
            x = x + s5_mixer(xn, j, p)
        else:
            x = x + conv_mixer(xn, j, p)
        x = x + cross_attention(rmsnorm(x, p['norm_xq'][i]), rmsnorm(mem, p['norm_mem'][i]), i, p)
        x = x + sqrelu_mlp(rmsnorm(x, p['norm_ffn'][i]), i, p)
    return rmsnorm(x, p['norm_final'])


def setup_inputs(seed: int = 0) -> dict:
    key = jax.random.key(seed)
    ks = iter(jax.random.split(key, 40))
    f32 = jnp.float32

    def nrm(shape, scale):
        return jax.random.normal(next(ks), shape, f32) * scale

    def gain(shape):
        return 1.0 + nrm(shape, 0.02)

    D, G, P, GC = D_MODEL, N_GROUPS, STATE, GROUP_CH
    NA, NB = N_SSM_LAYERS, N_CONV_LAYERS
    n_idx = jnp.arange(P, dtype=f32)
    inp = {}
    inp['x_prompt'] = nrm((BATCH, SEQ, D), 1.0)
    inp['x_sample'] = nrm((DEC_BATCH, DEC_SEQ, D), 1.0)
    inp['mem_prompt'] = nrm((BATCH, N_MEM, D), 1.0)
    inp['mem_sample'] = nrm((DEC_BATCH, N_MEM, D), 1.0)
    inp['norm_mix'] = gain((DEPTH, D))
    inp['norm_xq'] = gain((DEPTH, D))
    inp['norm_mem'] = gain((DEPTH, D))
    inp['norm_ffn'] = gain((DEPTH, D))
    inp['norm_final'] = gain((D,))
    inp['ssm_a_re'] = -0.5 + nrm((NA, 2, G, P), 0.01)
    inp['ssm_a_im'] = math.pi * n_idx + nrm((NA, 2, G, P), 0.01)
    inp['ssm_log_step'] = jax.random.uniform(next(ks), (NA, 2, G), f32,
                                             math.log(DT_MIN), math.log(DT_MAX))
    inp['ssm_b_re'] = nrm((NA, 2, G, P, GC), (2 * GC) ** -0.5)
    inp['ssm_b_im'] = nrm((NA, 2, G, P, GC), (2 * GC) ** -0.5)
    inp['ssm_c_re'] = nrm((NA, 2, G, GC, P), P ** -0.5)
    inp['ssm_c_im'] = nrm((NA, 2, G, GC, P), P ** -0.5)
    inp['ssm_d'] = nrm((NA, D), 1.0)
    inp['ssm_w_glu'] = nrm((NA, D, 2 * D), D ** -0.5)
    inp['ssm_b_glu'] = nrm((NA, 2 * D), 0.01)
    inp['conv_w_in'] = nrm((NB, D, 2 * D), D ** -0.5)
    inp['conv_b_in'] = nrm((NB, 2 * D), 0.01)
    inp['conv_w_dw'] = nrm((NB, CONV_WIDTH, D), CONV_WIDTH ** -0.5)
    inp['conv_b_dw'] = nrm((NB, D), 0.01)
    inp['conv_ln_g'] = gain((NB, D))
    inp['conv_ln_b'] = nrm((NB, D), 0.01)
    inp['conv_w_out'] = nrm((NB, D, D), D ** -0.5)
    inp['conv_b_out'] = nrm((NB, D), 0.01)
    inp['attn_w_q'] = nrm((DEPTH, D, D), D ** -0.5)
    inp['attn_w_kv'] = nrm((DEPTH, D, 2 * D), D ** -0.5)
    inp['attn_w_o'] = nrm((DEPTH, D, D), D ** -0.5)
    inp['mlp_w_in'] = nrm((DEPTH, D, D_FF), D ** -0.5)
    inp['mlp_w_out'] = nrm((DEPTH, D_FF, D), D_FF ** -0.5)
    return inp


def reference(x_prompt, x_sample, mem_prompt, mem_sample,
              norm_mix, norm_xq, norm_mem, norm_ffn, norm_final,
              ssm_a_re, ssm_a_im, ssm_log_step, ssm_b_re, ssm_b_im, ssm_c_re, ssm_c_im,
              ssm_d, ssm_w_glu, ssm_b_glu,
              conv_w_in, conv_b_in, conv_w_dw, conv_b_dw, conv_ln_g, conv_ln_b, conv_w_out, conv_b_out,
              attn_w_q, attn_w_kv, attn_w_o,
              mlp_w_in, mlp_w_out):
    p = dict(norm_mix=norm_mix, norm_xq=norm_xq, norm_mem=norm_mem, norm_ffn=norm_ffn,
             norm_final=norm_final,
             ssm_a_re=ssm_a_re, ssm_a_im=ssm_a_im, ssm_log_step=ssm_log_step,
             ssm_b_re=ssm_b_re, ssm_b_im=ssm_b_im, ssm_c_re=ssm_c_re, ssm_c_im=ssm_c_im,
             ssm_d=ssm_d, ssm_w_glu=ssm_w_glu, ssm_b_glu=ssm_b_glu,
             conv_w_in=conv_w_in, conv_b_in=conv_b_in, conv_w_dw=conv_w_dw, conv_b_dw=conv_b_dw,
             conv_ln_g=conv_ln_g, conv_ln_b=conv_ln_b, conv_w_out=conv_w_out, conv_b_out=conv_b_out,
             attn_w_q=attn_w_q, attn_w_kv=attn_w_kv, attn_w_o=attn_w_o,
             mlp_w_in=mlp_w_in, mlp_w_out=mlp_w_out)
    y_prompt = trunk(x_prompt, mem_prompt, p)
    y_sample = trunk(x_sample, mem_sample, p)
    return (y_prompt, y_sample)
```

```python
import functools
import math

import jax
import jax.numpy as jnp
from jax import lax
from jax.experimental import pallas as pl
from jax.experimental.pallas import tpu as pltpu

D_MODEL = 1024
GROUP_CH = 16
N_GROUPS = D_MODEL // GROUP_CH
STATE = 64
CONV_WIDTH = 31
CONV_PAD = (CONV_WIDTH - 1) // 2
N_XHEADS = 4
XHEAD_DIM = D_MODEL // N_XHEADS
D_FF = 4 * D_MODEL
EPS = 1e-6

LANES = 128
N_SLABS = D_MODEL // LANES
GROUPS_PER_SLAB = LANES // GROUP_CH
S5_CHUNK = 8
SLAB_COLS = S5_CHUNK * LANES
SLAB_STATE = 2 * GROUPS_PER_SLAB * STATE
HALO = 16
VMEM_LIMIT = 56 * 1024 * 1024

F32 = jnp.float32
BF16 = jnp.bfloat16


def _cparams(n_axes):
    return pltpu.CompilerParams(
        dimension_semantics=("arbitrary",) * n_axes, vmem_limit_bytes=VMEM_LIMIT)


def _const_spec(shape):
    nd = len(shape)
    return pl.BlockSpec(shape, lambda *_: (0,) * nd, pipeline_mode=pl.Buffered(1))


def _rms(x, g):
    return x * lax.rsqrt(jnp.mean(x * x, axis=-1, keepdims=True) + EPS) * g


def _token_tile(n):
    t = min(512, n)
    assert n % t == 0
    return t


def _kv_kernel(mem_ref, g_ref, w_ref, k_ref, v_ref):
    mn = _rms(mem_ref[0], g_ref[0]).astype(BF16)
    kv = jnp.dot(mn, w_ref[0], preferred_element_type=F32)
    k_ref[0, 0] = kv[:, :D_MODEL].astype(BF16)
    v_ref[0, 0] = kv[:, D_MODEL:].astype(BF16)


def _kv_all_layers(mem, norm_mem, w_kv):
    bsz, n_mem, _ = mem.shape
    depth = w_kv.shape[0]
    out = jax.ShapeDtypeStruct((depth, bsz, n_mem, D_MODEL), BF16)
    return pl.pallas_call(
        _kv_kernel,
        grid=(depth, bsz),
        in_specs=[
            pl.BlockSpec((1, n_mem, D_MODEL), lambda i, b: (b, 0, 0)),
            pl.BlockSpec((1, 1, D_MODEL), lambda i, b: (i, 0, 0)),
            pl.BlockSpec((1, D_MODEL, 2 * D_MODEL), lambda i, b: (i, 0, 0)),
        ],
        out_specs=[
            pl.BlockSpec((1, 1, n_mem, D_MODEL), lambda i, b: (i, b, 0, 0)),
            pl.BlockSpec((1, 1, n_mem, D_MODEL), lambda i, b: (i, b, 0, 0)),
        ],
        out_shape=[out, out],
        compiler_params=_cparams(2),
        name="kv_proj",
    )(mem, norm_mem.reshape(depth, 1, D_MODEL), w_kv)


def _rms_slab_kernel(x_ref, g_ref, o_ref):
    xn = _rms(x_ref[...], g_ref[...]).astype(BF16)
    for s in range(N_SLABS):
        o_ref[s] = xn[:, s * LANES:(s + 1) * LANES]


def _rms_to_slabs(x2, g):
    n_tok = x2.shape[0]
    tt = _token_tile(n_tok)
    return pl.pallas_call(
        _rms_slab_kernel,
        grid=(n_tok // tt,),
        in_specs=[pl.BlockSpec((tt, D_MODEL), lambda t: (t, 0)), _const_spec((1, D_MODEL))],
        out_specs=pl.BlockSpec((N_SLABS, tt, LANES), lambda t: (0, t, 0)),
        out_shape=jax.ShapeDtypeStruct((N_SLABS, n_tok, LANES), BF16),
        compiler_params=_cparams(1),
        name="rms_slabs",
    )(x2, g.reshape(1, D_MODEL))


def _s5_kernel(u_ref, win_ref, wy_ref, lam_ref, d_ref, o_ref, sf_ref, sb_ref, *, bb, nc, rt):
    rows = bb * nc
    nblk = SLAB_STATE // LANES
    hblk = nblk // 2
    for r0 in range(0, rows, rt):
        s = jnp.dot(u_ref[0, r0:r0 + rt, :], win_ref[0], preferred_element_type=F32)
        for c in range(nblk):
            sf_ref[c, r0:r0 + rt, :] = s[:, c * LANES:(c + 1) * LANES]
            sb_ref[c, r0:r0 + rt, :] = s[:, SLAB_STATE + c * LANES:SLAB_STATE + (c + 1) * LANES]

    lam = lam_ref[0]

    def rows_of_chunk(k):
        return pl.ds(k, bb, stride=nc) if bb > 1 else pl.ds(k, 1)

    def advance(s_ref, lam_row, idx, carry):
        new = [None] * nblk
        for c in range(hblk):
            l_re = lam_row[:, c * LANES:(c + 1) * LANES]
            l_im = lam_row[:, (hblk + c) * LANES:(hblk + c + 1) * LANES]
            s_re, s_im = s_ref[c, idx, :], s_ref[hblk + c, idx, :]
            c_re, c_im = carry[c], carry[hblk + c]
            s_ref[c, idx, :] = c_re
            s_ref[hblk + c, idx, :] = c_im
            new[c] = l_re * c_re - l_im * c_im + s_re
            new[hblk + c] = l_re * c_im + l_im * c_re + s_im
        return tuple(new)

    def body(k, carry):
        cf, cb = carry
        return (advance(sf_ref, lam[0:1], rows_of_chunk(k), cf),
                advance(sb_ref, lam[1:2], rows_of_chunk(nc - 1 - k), cb))

    zero = tuple(jnp.zeros((bb, LANES), F32) for _ in range(nblk))
    lax.fori_loop(0, nc, body, (zero, zero))

    def entering(s_ref, r0):
        return jnp.concatenate([s_ref[c, r0:r0 + rt, :] for c in range(nblk)], axis=-1).astype(BF16)

    d = d_ref[0]
    for r0 in range(0, rows, rt):
        u = u_ref[0, r0:r0 + rt, :]
        y = jnp.dot(u, wy_ref[0, 0:SLAB_COLS, :], preferred_element_type=F32)
        y = y + jnp.dot(entering(sf_ref, r0),
                        wy_ref[0, SLAB_COLS:SLAB_COLS + SLAB_STATE, :], preferred_element_type=F32)
        y = y + jnp.dot(entering(sb_ref, r0),
                        wy_ref[0, SLAB_COLS + SLAB_STATE:, :], preferred_element_type=F32)
        y = y + d * u.astype(F32)
        o_ref[0, r0:r0 + rt, :] = jax.nn.gelu(y).astype(BF16)


def _s5_prepare(a_re, a_im, log_step, b_re, b_im, c_re, c_im, d_skip):
    t_len, q_n = S5_CHUNK, GROUPS_PER_SLAB
    step = jnp.exp(log_step.astype(F32))[..., None]
    ar, ai = a_re.astype(F32) * step, a_im.astype(F32) * step
    n = jnp.arange(t_len + 1, dtype=F32)[None, :, None, None]
    mag = jnp.exp(ar[:, None] * n)
    pw_re, pw_im = mag * jnp.cos(ai[:, None] * n), mag * jnp.sin(ai[:, None] * n)
    num_re, num_im = pw_re[:, 1] - 1.0, pw_im[:, 1]
    den = a_re * a_re + a_im * a_im
    z_re = (num_re * a_re + num_im * a_im) / den
    z_im = (num_im * a_re - num_re * a_im) / den
    bb_re = z_re[..., None] * b_re - z_im[..., None] * b_im
    bb_im = z_re[..., None] * b_im + z_im[..., None] * b_re

    def lam_pow_b(d, idx):
        pr, pi = pw_re[d][idx], pw_im[d][idx]
        return (pr[..., None] * bb_re[d][None] - pi[..., None] * bb_im[d][None],
                pr[..., None] * bb_im[d][None] + pi[..., None] * bb_re[d][None])

    def c_lam_pow(d, idx):
        pr, pi = pw_re[d][idx][:, :, None, :], pw_im[d][idx][:, :, None, :]
        return (c_re[d][None] * pr - c_im[d][None] * pi, c_re[d][None] * pi + c_im[d][None] * pr)

    eye = jnp.eye(q_n, dtype=F32)
    t_idx = jnp.arange(t_len)

    def embed_in(x_re, x_im):
        x = jnp.stack([x_re, x_im], 0).reshape(2, t_len, N_SLABS, q_n, STATE, GROUP_CH)
        w = jnp.einsum('itsqpc,qr->stqcirp', x, eye)
        return w.reshape(N_SLABS, SLAB_COLS, SLAB_STATE)

    def embed_out(m_re, m_im):
        x = jnp.stack([m_re, -m_im], 0).reshape(2, t_len, N_SLABS, q_n, GROUP_CH, STATE)
        w = jnp.einsum('itsqcp,qr->siqptrc', x, eye)
        return w.reshape(N_SLABS, SLAB_STATE, SLAB_COLS)

    w_in = jnp.concatenate([embed_in(*lam_pow_b(0, t_len - 1 - t_idx)),
                            embed_in(*lam_pow_b(1, t_idx))], axis=-1)
    w_out_f = embed_out(*c_lam_pow(0, t_idx + 1))
    w_out_b = embed_out(*c_lam_pow(1, t_len - t_idx))

    def impulse(d):
        lr, li = lam_pow_b(d, t_idx)
        return (jnp.einsum('gcp,ngpk->ngck', c_re[d], lr)
                - jnp.einsum('gcp,ngpk->ngck', c_im[d], li))

    k_f, k_b = impulse(0), impulse(1)
    delta = t_idx[None, :] - t_idx[:, None]
    fwd = jnp.where((delta >= 0)[..., None, None, None], k_f[jnp.clip(delta, 0)], 0.0)
    bwd = jnp.where((delta <= 0)[..., None, None, None], k_b[jnp.clip(-delta, 0)], 0.0)
    toe = (fwd + bwd).reshape(t_len, t_len, N_SLABS, q_n, GROUP_CH, GROUP_CH)
    w_intra = jnp.einsum('tusqck,qr->stqkurc', toe, eye).reshape(N_SLABS, SLAB_COLS, SLAB_COLS)

    w_y = jnp.concatenate([w_intra, w_out_f, w_out_b], axis=1)

    def slab_state(x_re, x_im):
        return jnp.concatenate([x_re.reshape(N_SLABS, -1), x_im.reshape(N_SLABS, -1)], axis=-1)

    lam_t = jnp.stack([slab_state(pw_re[0, t_len], pw_im[0, t_len]),
                       slab_state(pw_re[1, t_len], pw_im[1, t_len])], axis=1)
    d = jnp.tile(d_skip.astype(F32).reshape(N_SLABS, 1, LANES), (1, 1, t_len))
    return w_in.astype(BF16), w_y.astype(BF16), lam_t, d


def _s5_scan(u_slabs, seq, prep):
    w_in, w_y, lam_t, d = prep
    n_tok = u_slabs.shape[1]
    nc = seq // S5_CHUNK
    n_rows = n_tok // S5_CHUNK
    n_seq = n_rows // nc
    bb = max(1, min(n_seq, 1024 // nc))
    assert n_seq % bb == 0
    rows = bb * nc
    rt = min(256, rows)
    assert rows % rt == 0
    u = u_slabs.reshape(N_SLABS, n_rows, SLAB_COLS)
    out = pl.pallas_call(
        functools.partial(_s5_kernel, bb=bb, nc=nc, rt=rt),
        grid=(N_SLABS, n_rows // rows),
        in_specs=[
            pl.BlockSpec((1, rows, SLAB_COLS), lambda s, r: (s, r, 0)),
            pl.BlockSpec((1, SLAB_COLS, 2 * SLAB_STATE), lambda s, r: (s, 0, 0)),
            pl.BlockSpec((1, SLAB_COLS + 2 * SLAB_STATE, SLAB_COLS), lambda s, r: (s, 0, 0)),
            pl.BlockSpec((1, 2, SLAB_STATE), lambda s, r: (s, 0, 0)),
            pl.BlockSpec((1, 1, SLAB_COLS), lambda s, r: (s, 0, 0)),
        ],
        out_specs=pl.BlockSpec((1, rows, SLAB_COLS), lambda s, r: (s, r, 0)),
        out_shape=jax.ShapeDtypeStruct((N_SLABS, n_rows, SLAB_COLS), BF16),
        scratch_shapes=[pltpu.VMEM((SLAB_STATE // LANES, rows, LANES), F32)] * 2,
        compiler_params=_cparams(2),
        name="s5_scan",
    )(u, w_in, w_y, lam_t, d)
    return out.reshape(N_SLABS, n_tok, LANES)


def _glu_kernel(g_ref, x_ref, w_ref, b_ref, o_ref):
    g = jnp.concatenate([g_ref[s] for s in range(N_SLABS)], axis=-1)
    h = jnp.dot(g, w_ref[...], preferred_element_type=F32) + b_ref[...]
    o_ref[...] = x_ref[...] + h[:, :D_MODEL] * jax.nn.sigmoid(h[:, D_MODEL:])


def _glu_residual(g_slabs, x2, w, b):
    n_tok = x2.shape[0]
    tt = _token_tile(n_tok)
    return pl.pallas_call(
        _glu_kernel,
        grid=(n_tok // tt,),
        in_specs=[
            pl.BlockSpec((N_SLABS, tt, LANES), lambda t: (0, t, 0)),
            pl.BlockSpec((tt, D_MODEL), lambda t: (t, 0)),
            _const_spec((D_MODEL, 2 * D_MODEL)),
            _const_spec((1, 2 * D_MODEL)),
        ],
        out_specs=pl.BlockSpec((tt, D_MODEL), lambda t: (t, 0)),
        out_shape=jax.ShapeDtypeStruct((n_tok, D_MODEL), F32),
        compiler_params=_cparams(1),
        name="s5_glu",
    )(g_slabs, x2, w, b.reshape(1, 2 * D_MODEL))


def _conv_in_kernel(x_ref, g_ref, w_ref, b_ref, o_ref):
    xn = _rms(x_ref[...], g_ref[...]).astype(BF16)
    h = jnp.dot(xn, w_ref[...], preferred_element_type=F32) + b_ref[...]
    o_ref[...] = (h[:, :D_MODEL] * jax.nn.sigmoid(h[:, D_MODEL:])).astype(BF16)


def _conv_in(x2, g, w, b):
    n_tok = x2.shape[0]
    tt = _token_tile(n_tok)
    return pl.pallas_call(
        _conv_in_kernel,
        grid=(n_tok // tt,),
        in_specs=[
            pl.BlockSpec((tt, D_MODEL), lambda t: (t, 0)),
            _const_spec((1, D_MODEL)),
            _const_spec((D_MODEL, 2 * D_MODEL)),
            _const_spec((1, 2 * D_MODEL)),
        ],
        out_specs=pl.BlockSpec((tt, D_MODEL), lambda t: (t, 0)),
        out_shape=jax.ShapeDtypeStruct((n_tok, D_MODEL), BF16),
        compiler_params=_cparams(1),
        name="conv_in",
    )(x2, g.reshape(1, D_MODEL), w, b.reshape(1, 2 * D_MODEL))


def _conv_out_kernel(u_ref, up_ref, un_ref, x_ref, wdw_ref, bdw_ref, lg_ref, lb_ref, w_ref, b_ref,
                     o_ref, buf_ref, cv_ref, *, tt, rb):
    t = pl.program_id(1)
    last = pl.num_programs(1) - 1
    buf_ref[0:HALO, :] = jnp.where(t > 0, up_ref[0].astype(F32), 0.0)
    buf_ref[HALO:HALO + tt, :] = u_ref[0].astype(F32)
    buf_ref[HALO + tt:, :] = jnp.where(t < last, un_ref[0].astype(F32), 0.0)
    off = HALO - CONV_PAD
    for r0 in range(0, tt, rb):
        acc = jnp.broadcast_to(bdw_ref[...], (rb, D_MODEL))
        for k in range(CONV_WIDTH):
            acc = acc + wdw_ref[k:k + 1, :] * buf_ref[r0 + off + k:r0 + off + k + rb, :]
        cv_ref[r0:r0 + rb, :] = acc
    h = cv_ref[...]
    mu = jnp.mean(h, axis=-1, keepdims=True)
    hc = h - mu
    var = jnp.mean(hc * hc, axis=-1, keepdims=True)
    y = hc * lax.rsqrt(var + EPS) * lg_ref[...] + lb_ref[...]
    y = jax.nn.silu(y).astype(BF16)
    o_ref[0] = x_ref[0] + jnp.dot(y, w_ref[...], preferred_element_type=F32) + b_ref[...]


def _conv_out(u, x, w_dw, b_dw, ln_g, ln_b, w_out, b_out):
    bsz, seq, _ = x.shape
    tt = min(256, seq)
    assert seq % tt == 0 and tt % HALO == 0
    rb = min(32, tt)
    n_halo = seq // HALO
    per = tt // HALO
    row = lambda v: v.reshape(1, D_MODEL)
    return pl.pallas_call(
        functools.partial(_conv_out_kernel, tt=tt, rb=rb),
        grid=(bsz, seq // tt),
        in_specs=[
            pl.BlockSpec((1, tt, D_MODEL), lambda b, t: (b, t, 0)),
            pl.BlockSpec((1, HALO, D_MODEL), lambda b, t: (b, jnp.maximum(t * per - 1, 0), 0)),
            pl.BlockSpec((1, HALO, D_MODEL), lambda b, t: (b, jnp.minimum((t + 1) * per, n_halo - 1), 0)),
            pl.BlockSpec((1, tt, D_MODEL), lambda b, t: (b, t, 0)),
            _const_spec((CONV_WIDTH, D_MODEL)),
            _const_spec((1, D_MODEL)),
            _const_spec((1, D_MODEL)),
            _const_spec((1, D_MODEL)),
            _const_spec((D_MODEL, D_MODEL)),
            _const_spec((1, D_MODEL)),
        ],
        out_specs=pl.BlockSpec((1, tt, D_MODEL), lambda b, t: (b, t, 0)),
        out_shape=jax.ShapeDtypeStruct(x.shape, F32),
        scratch_shapes=[pltpu.VMEM((tt + 2 * HALO, D_MODEL), F32), pltpu.VMEM((tt, D_MODEL), F32)],
        compiler_params=_cparams(2),
        name="conv_out",
    )(u, u, u, x, w_dw, row(b_dw), row(ln_g), row(ln_b), w_out, row(b_out))


def _attn_kernel(x_ref, k_ref, v_ref, g_ref, wq_ref, wo_ref, o_ref):
    x = x_ref[0]
    xn = _rms(x, g_ref[...]).astype(BF16)
    q = jnp.dot(xn, wq_ref[...], preferred_element_type=F32)
    q = (q * (XHEAD_DIM ** -0.5)).astype(BF16)
    heads = []
    for h in range(N_XHEADS):
        sl = slice(h * XHEAD_DIM, (h + 1) * XHEAD_DIM)
        s = lax.dot_general(q[:, sl], k_ref[0, 0, :, sl], (((1,), (1,)), ((), ())),
                            preferred_element_type=F32)
        e = jnp.exp(s - jnp.max(s, axis=-1, keepdims=True))
        p = (e / jnp.sum(e, axis=-1, keepdims=True)).astype(BF16)
        heads.append(jnp.dot(p, v_ref[0, 0, :, sl], preferred_element_type=F32).astype(BF16))
    o = jnp.concatenate(heads, axis=-1)
    o_ref[0] = x + jnp.dot(o, wo_ref[...], preferred_element_type=F32)


def _cross_attention(x, k_all, v_all, layer, g, wq, wo):
    bsz, seq, _ = x.shape
    n_mem = k_all.shape[2]
    tt = _token_tile(seq)
    kv_spec = pl.BlockSpec((1, 1, n_mem, D_MODEL), lambda b, t: (layer, b, 0, 0))
    return pl.pallas_call(
        _attn_kernel,
        grid=(bsz, seq // tt),
        in_specs=[
            pl.BlockSpec((1, tt, D_MODEL), lambda b, t: (b, t, 0)),
            kv_spec, kv_spec,
            _const_spec((1, D_MODEL)),
            _const_spec((D_MODEL, D_MODEL)),
            _const_spec((D_MODEL, D_MODEL)),
        ],
        out_specs=pl.BlockSpec((1, tt, D_MODEL), lambda b, t: (b, t, 0)),
        out_shape=jax.ShapeDtypeStruct(x.shape, F32),
        compiler_params=_cparams(2),
        name="cross_attn",
    )(x, k_all, v_all, g.reshape(1, D_MODEL), wq, wo)


def _mlp_kernel(x_ref, g_ref, win_ref, wout_ref, gf_ref, o_ref, *, final):
    x = x_ref[...]
    xn = _rms(x, g_ref[...]).astype(BF16)
    acc = x
    for c in range(D_FF // D_MODEL):
        sl = slice(c * D_MODEL, (c + 1) * D_MODEL)
        h = jnp.dot(xn, win_ref[:, sl], preferred_element_type=F32)
        h = jnp.square(jnp.maximum(h, 0.0)).astype(BF16)
        acc = acc + jnp.dot(h, wout_ref[sl, :], preferred_element_type=F32)
    o_ref[...] = _rms(acc, gf_ref[...]) if final else acc


def _mlp(x2, g, w_in, w_out, g_final, final):
    n_tok = x2.shape[0]
    tt = _token_tile(n_tok)
    return pl.pallas_call(
        functools.partial(_mlp_kernel, final=final),
        grid=(n_tok // tt,),
        in_specs=[
            pl.BlockSpec((tt, D_MODEL), lambda t: (t, 0)),
            _const_spec((1, D_MODEL)),
            _const_spec((D_MODEL, D_FF)),
            _const_spec((D_FF, D_MODEL)),
            _const_spec((1, D_MODEL)),
        ],
        out_specs=pl.BlockSpec((tt, D_MODEL), lambda t: (t, 0)),
        out_shape=jax.ShapeDtypeStruct((n_tok, D_MODEL), F32),
        compiler_params=_cparams(1),
        name="mlp_final" if final else "mlp",
    )(x2, g.reshape(1, D_MODEL), w_in, w_out, g_final.reshape(1, D_MODEL))


def _trunk(x, mem, p, s5_preps):
    bsz, seq, _ = x.shape
    depth = p['attn_w_q'].shape[0]
    k_all, v_all = _kv_all_layers(mem, p['norm_mem'], p['attn_w_kv'])
    flat = lambda a: a.reshape(bsz * seq, D_MODEL)
    for i in range(depth):
        j = i // 2
        if i % 2 == 0:
            u = _rms_to_slabs(flat(x), p['norm_mix'][i])
            g = _s5_scan(u, seq, s5_preps[j])
            x = _glu_residual(g, flat(x), p['ssm_w_glu'][j], p['ssm_b_glu'][j]).reshape(x.shape)
        else:
            u = _conv_in(flat(x), p['norm_mix'][i], p['conv_w_in'][j], p['conv_b_in'][j])
            x = _conv_out(u.reshape(x.shape), x, p['conv_w_dw'][j], p['conv_b_dw'][j],
                          p['conv_ln_g'][j], p['conv_ln_b'][j], p['conv_w_out'][j], p['conv_b_out'][j])
        x = _cross_attention(x, k_all, v_all, i, p['norm_xq'][i], p['attn_w_q'][i], p['attn_w_o'][i])
        x = _mlp(flat(x), p['norm_ffn'][i], p['mlp_w_in'][i], p['mlp_w_out'][i],
                 p['norm_final'], i == depth - 1).reshape(x.shape)
    return x


def kernel(x_prompt, x_sample, mem_prompt, mem_sample, norm_mix, norm_xq, norm_mem, norm_ffn, norm_final, ssm_a_re, ssm_a_im, ssm_log_step, ssm_b_re, ssm_b_im, ssm_c_re, ssm_c_im, ssm_d, ssm_w_glu, ssm_b_glu, conv_w_in, conv_b_in, conv_w_dw, conv_b_dw, conv_ln_g, conv_ln_b, conv_w_out, conv_b_out, attn_w_q, attn_w_kv, attn_w_o, mlp_w_in, mlp_w_out):
    p = dict(norm_mix=norm_mix, norm_xq=norm_xq, norm_mem=norm_mem, norm_ffn=norm_ffn,
             norm_final=norm_final, ssm_b_glu=ssm_b_glu,
             conv_b_in=conv_b_in, conv_w_dw=conv_w_dw, conv_b_dw=conv_b_dw,
             conv_ln_g=conv_ln_g, conv_ln_b=conv_ln_b, conv_b_out=conv_b_out)
    for name, w in (('ssm_w_glu', ssm_w_glu), ('conv_w_in', conv_w_in), ('conv_w_out', conv_w_out),
                    ('attn_w_q', attn_w_q), ('attn_w_kv', attn_w_kv), ('attn_w_o', attn_w_o),
                    ('mlp_w_in', mlp_w_in), ('mlp_w_out', mlp_w_out)):
        p[name] = w.astype(BF16)
    s5_preps = [_s5_prepare(ssm_a_re[j], ssm_a_im[j], ssm_log_step[j], ssm_b_re[j], ssm_b_im[j],
                            ssm_c_re[j], ssm_c_im[j], ssm_d[j]) for j in range(ssm_a_re.shape[0])]
    y_prompt = _trunk(x_prompt, mem_prompt, p, s5_preps)
    y_sample = _trunk(x_sample, mem_sample, p, s5_preps)
    return (y_prompt, y_sample)
```

```python
import functools
import math

import jax
import jax.numpy as jnp
from jax import lax
from jax.experimental import pallas as pl
from jax.experimental.pallas import tpu as pltpu

D_MODEL = 1024
GROUP_CH = 16
N_GROUPS = D_MODEL // GROUP_CH
STATE = 64
CONV_WIDTH = 31
CONV_PAD = (CONV_WIDTH - 1) // 2
N_XHEADS = 4
XHEAD_DIM = D_MODEL // N_XHEADS
D_FF = 4 * D_MODEL
EPS = 1e-6

LANES = 128
N_SLABS = D_MODEL // LANES
GROUPS_PER_SLAB = LANES // GROUP_CH
S5_CHUNK = 8
SLAB_COLS = S5_CHUNK * LANES
SLAB_STATE = 2 * GROUPS_PER_SLAB * STATE
HALO = 16
VMEM_LIMIT = 56 * 1024 * 1024

F32 = jnp.float32
BF16 = jnp.bfloat16


def _cparams(n_axes):
    return pltpu.CompilerParams(
        dimension_semantics=("arbitrary",) * n_axes, vmem_limit_bytes=VMEM_LIMIT)


def _const_spec(shape):
    nd = len(shape)
    return pl.BlockSpec(shape, lambda *_: (0,) * nd, pipeline_mode=pl.Buffered(1))


def _rms(x, g):
    return x * lax.rsqrt(jnp.mean(x * x, axis=-1, keepdims=True) + EPS) * g


def _token_tile(n):
    t = min(512, n)
    assert n % t == 0
    return t


def _kv_kernel(mem_ref, g_ref, w_ref, k_ref, v_ref):
    mn = _rms(mem_ref[0], g_ref[0]).astype(BF16)
    kv = jnp.dot(mn, w_ref[0], preferred_element_type=F32)
    k_ref[0, 0] = kv[:, :D_MODEL].astype(BF16)
    v_ref[0, 0] = kv[:, D_MODEL:].astype(BF16)


def _kv_all_layers(mem, norm_mem, w_kv):
    bsz, n_mem, _ = mem.shape
    depth = w_kv.shape[0]
    out = jax.ShapeDtypeStruct((depth, bsz, n_mem, D_MODEL), BF16)
    return pl.pallas_call(
        _kv_kernel,
        grid=(depth, bsz),
        in_specs=[
            pl.BlockSpec((1, n_mem, D_MODEL), lambda i, b: (b, 0, 0)),
            pl.BlockSpec((1, 1, D_MODEL), lambda i, b: (i, 0, 0)),
            pl.BlockSpec((1, D_MODEL, 2 * D_MODEL), lambda i, b: (i, 0, 0)),
        ],
        out_specs=[
            pl.BlockSpec((1, 1, n_mem, D_MODEL), lambda i, b: (i, b, 0, 0)),
            pl.BlockSpec((1, 1, n_mem, D_MODEL), lambda i, b: (i, b, 0, 0)),
        ],
        out_shape=[out, out],
        compiler_params=_cparams(2),
        name="kv_proj",
    )(mem, norm_mem.reshape(depth, 1, D_MODEL), w_kv)


def _rms_slab_kernel(x_ref, g_ref, o_ref):
    xn = _rms(x_ref[...], g_ref[...]).astype(BF16)
    for s in range(N_SLABS):
        o_ref[s] = xn[:, s * LANES:(s + 1) * LANES]


def _rms_to_slabs(x2, g):
    n_tok = x2.shape[0]
    tt = _token_tile(n_tok)
    return pl.pallas_call(
        _rms_slab_kernel,
        grid=(n_tok // tt,),
        in_specs=[pl.BlockSpec((tt, D_MODEL), lambda t: (t, 0)), _const_spec((1, D_MODEL))],
        out_specs=pl.BlockSpec((N_SLABS, tt, LANES), lambda t: (0, t, 0)),
        out_shape=jax.ShapeDtypeStruct((N_SLABS, n_tok, LANES), BF16),
        compiler_params=_cparams(1),
        name="rms_slabs",
    )(x2, g.reshape(1, D_MODEL))


def _s5_kernel(u_ref, win_ref, wi_ref, wof_ref, wob_ref, lam_ref, d_ref, o_ref,
               ulhs_ref, stage_ref, sf_ref, sb_ref, *, bb, nc, rt, pitch):
    nblk = SLAB_STATE // LANES
    hblk = nblk // 2
    tiles = [(b * nc + k0, b * pitch + k0) for b in range(bb) for k0 in range(0, nc, rt)]
    for r0, p0 in tiles:
        stage_ref[...] = u_ref[0, r0 * S5_CHUNK:(r0 + rt) * S5_CHUNK, :].astype(F32)
        u = jnp.concatenate([stage_ref[pl.ds(t, rt, stride=S5_CHUNK), :].astype(BF16)
                             for t in range(S5_CHUNK)], axis=-1)
        ulhs_ref[r0:r0 + rt, :] = u
        s = jnp.dot(u, win_ref[0], preferred_element_type=F32)
        for c in range(nblk):
            sf_ref[c, p0:p0 + rt, :] = s[:, c * LANES:(c + 1) * LANES]
            sb_ref[c, p0:p0 + rt, :] = s[:, SLAB_STATE + c * LANES:SLAB_STATE + (c + 1) * LANES]

    lam = lam_ref[0]

    def rows_of_chunk(k):
        return pl.ds(k, bb, stride=pitch) if bb > 1 else pl.ds(k, 1)

    def advance(s_ref, lam_row, idx, carry):
        new = [None] * nblk
        for c in range(hblk):
            l_re = lam_row[:, c * LANES:(c + 1) * LANES]
            l_im = lam_row[:, (hblk + c) * LANES:(hblk + c + 1) * LANES]
            s_re, s_im = s_ref[c, idx, :], s_ref[hblk + c, idx, :]
            c_re, c_im = carry[c], carry[hblk + c]
            s_ref[c, idx, :] = c_re
            s_ref[hblk + c, idx, :] = c_im
            new[c] = l_re * c_re - l_im * c_im + s_re
            new[hblk + c] = l_re * c_im + l_im * c_re + s_im
        return tuple(new)

    def body(k, carry):
        cf, cb = carry
        return (advance(sf_ref, lam[0:1], rows_of_chunk(k), cf),
                advance(sb_ref, lam[1:2], rows_of_chunk(nc - 1 - k), cb))

    zero = tuple(jnp.zeros((bb, LANES), F32) for _ in range(nblk))
    lax.fori_loop(0, nc, body, (zero, zero))

    def entering(s_ref, p0):
        return jnp.concatenate([s_ref[c, p0:p0 + rt, :] for c in range(nblk)], axis=-1).astype(BF16)

    d = d_ref[0]
    for r0, p0 in tiles:
        u = ulhs_ref[r0:r0 + rt, :]
        y = jnp.dot(u, wi_ref[0], preferred_element_type=F32)
        y = y + jnp.dot(entering(sf_ref, p0), wof_ref[0], preferred_element_type=F32)
        y = y + jnp.dot(entering(sb_ref, p0), wob_ref[0], preferred_element_type=F32)
        g = jax.nn.gelu(y + d * u.astype(F32))
        for t in range(S5_CHUNK):
            stage_ref[pl.ds(t, rt, stride=S5_CHUNK), :] = g[:, t * LANES:(t + 1) * LANES]
        o_ref[0, r0 * S5_CHUNK:(r0 + rt) * S5_CHUNK, :] = stage_ref[...].astype(BF16)


def _s5_prepare(a_re, a_im, log_step, b_re, b_im, c_re, c_im, d_skip):
    t_len, q_n = S5_CHUNK, GROUPS_PER_SLAB
    step = jnp.exp(log_step.astype(F32))[..., None]
    ar, ai = a_re.astype(F32) * step, a_im.astype(F32) * step
    n = jnp.arange(t_len + 1, dtype=F32)[None, :, None, None]
    mag = jnp.exp(ar[:, None] * n)
    pw_re, pw_im = mag * jnp.cos(ai[:, None] * n), mag * jnp.sin(ai[:, None] * n)
    num_re, num_im = pw_re[:, 1] - 1.0, pw_im[:, 1]
    den = a_re * a_re + a_im * a_im
    z_re = (num_re * a_re + num_im * a_im) / den
    z_im = (num_im * a_re - num_re * a_im) / den
    bb_re = z_re[..., None] * b_re - z_im[..., None] * b_im
    bb_im = z_re[..., None] * b_im + z_im[..., None] * b_re

    def lam_pow_b(d, idx):
        pr, pi = pw_re[d][idx], pw_im[d][idx]
        return (pr[..., None] * bb_re[d][None] - pi[..., None] * bb_im[d][None],
                pr[..., None] * bb_im[d][None] + pi[..., None] * bb_re[d][None])

    def c_lam_pow(d, idx):
        pr, pi = pw_re[d][idx][:, :, None, :], pw_im[d][idx][:, :, None, :]
        return (c_re[d][None] * pr - c_im[d][None] * pi, c_re[d][None] * pi + c_im[d][None] * pr)

    t_idx = jnp.arange(t_len)

    def expand(base, col_group):
        s_n, a_n, r_n, n_cols = base.shape
        shape = (1, 1, q_n, 1, n_cols)
        q = lax.broadcasted_iota(jnp.int32, shape, 2)
        r = (lax.broadcasted_iota(jnp.int32, shape, 4) // col_group) % q_n
        w = jnp.where(q == r, base[:, :, None, :, :], 0.0)
        return w.reshape(s_n, a_n * q_n * r_n, n_cols).astype(BF16)

    def in_base(d, idx):
        x_re, x_im = lam_pow_b(d, idx)
        x = jnp.stack([x_re, x_im], 0).reshape(2, t_len, N_SLABS, q_n, STATE, GROUP_CH)
        return x.transpose(2, 1, 5, 0, 3, 4).reshape(N_SLABS, t_len, GROUP_CH, SLAB_STATE)

    def out_base(d, idx):
        m_re, m_im = c_lam_pow(d, idx)
        x = jnp.stack([m_re, -m_im], 0).reshape(2, t_len, N_SLABS, q_n, GROUP_CH, STATE)
        return x.transpose(2, 0, 5, 1, 3, 4).reshape(N_SLABS, 2, STATE, SLAB_COLS)

    w_in = expand(jnp.concatenate([in_base(0, t_len - 1 - t_idx), in_base(1, t_idx)], axis=-1), STATE)
    w_out_f = expand(out_base(0, t_idx + 1), GROUP_CH)
    w_out_b = expand(out_base(1, t_len - t_idx), GROUP_CH)

    def impulse(d):
        lr, li = lam_pow_b(d, t_idx)
        return (jnp.einsum('gcp,ngpk->ngck', c_re[d], lr)
                - jnp.einsum('gcp,ngpk->ngck', c_im[d], li))

    k_f, k_b = impulse(0), impulse(1)
    delta = t_idx[None, :] - t_idx[:, None]
    fwd = jnp.where((delta >= 0)[..., None, None, None], k_f[jnp.clip(delta, 0)], 0.0)
    bwd = jnp.where((delta <= 0)[..., None, None, None], k_b[jnp.clip(-delta, 0)], 0.0)
    toe = (fwd + bwd).reshape(t_len, t_len, N_SLABS, q_n, GROUP_CH, GROUP_CH)
    w_intra = expand(toe.transpose(2, 0, 5, 1, 3, 4).reshape(N_SLABS, t_len, GROUP_CH, SLAB_COLS), GROUP_CH)

    def slab_state(x_re, x_im):
        return jnp.concatenate([x_re.reshape(N_SLABS, -1), x_im.reshape(N_SLABS, -1)], axis=-1)

    lam_t = jnp.stack([slab_state(pw_re[0, t_len], pw_im[0, t_len]),
                       slab_state(pw_re[1, t_len], pw_im[1, t_len])], axis=1)
    d = jnp.tile(d_skip.astype(F32).reshape(N_SLABS, 1, LANES), (1, 1, t_len))
    return w_in, w_intra, w_out_f, w_out_b, lam_t, d


def _s5_scan(u_slabs, seq, prep):
    w_in, w_intra, w_out_f, w_out_b, lam_t, d = prep
    n_tok = u_slabs.shape[1]
    nc = seq // S5_CHUNK
    n_seq = n_tok // seq
    bb = max(1, min(n_seq, 1024 // nc))
    assert n_seq % bb == 0 and seq % S5_CHUNK == 0
    rows = bb * nc
    rt = min(256, nc)
    assert nc % rt == 0
    pitch = nc + 4 if nc % 8 == 0 else nc
    slab_w = lambda rows_, cols_: pl.BlockSpec((1, rows_, cols_), lambda s, r: (s, 0, 0))
    tok_spec = pl.BlockSpec((1, rows * S5_CHUNK, LANES), lambda s, r: (s, r, 0))
    state = pltpu.VMEM((SLAB_STATE // LANES, bb * pitch, LANES), F32)
    return pl.pallas_call(
        functools.partial(_s5_kernel, bb=bb, nc=nc, rt=rt, pitch=pitch),
        grid=(N_SLABS, n_seq // bb),
        in_specs=[
            tok_spec,
            slab_w(SLAB_COLS, 2 * SLAB_STATE),
            slab_w(SLAB_COLS, SLAB_COLS),
            slab_w(SLAB_STATE, SLAB_COLS),
            slab_w(SLAB_STATE, SLAB_COLS),
            slab_w(2, SLAB_STATE),
            slab_w(1, SLAB_COLS),
        ],
        out_specs=tok_spec,
        out_shape=jax.ShapeDtypeStruct(u_slabs.shape, BF16),
        scratch_shapes=[pltpu.VMEM((rows, SLAB_COLS), BF16),
                        pltpu.VMEM((rt * S5_CHUNK, LANES), F32), state, state],
        compiler_params=_cparams(2),
        name="s5_scan",
    )(u_slabs, w_in, w_intra, w_out_f, w_out_b, lam_t, d)


def _glu_kernel(g_ref, x_ref, w_ref, b_ref, o_ref):
    g = jnp.concatenate([g_ref[s] for s in range(N_SLABS)], axis=-1)
    h = jnp.dot(g, w_ref[...], preferred_element_type=F32) + b_ref[...]
    o_ref[...] = x_ref[...] + h[:, :D_MODEL] * jax.nn.sigmoid(h[:, D_MODEL:])


def _glu_residual(g_slabs, x2, w, b):
    n_tok = x2.shape[0]
    tt = _token_tile(n_tok)
    return pl.pallas_call(
        _glu_kernel,
        grid=(n_tok // tt,),
        in_specs=[
            pl.BlockSpec((N_SLABS, tt, LANES), lambda t: (0, t, 0)),
            pl.BlockSpec((tt, D_MODEL), lambda t: (t, 0)),
            _const_spec((D_MODEL, 2 * D_MODEL)),
            _const_spec((1, 2 * D_MODEL)),
        ],
        out_specs=pl.BlockSpec((tt, D_MODEL), lambda t: (t, 0)),
        out_shape=jax.ShapeDtypeStruct((n_tok, D_MODEL), F32),
        compiler_params=_cparams(1),
        name="s5_glu",
    )(g_slabs, x2, w, b.reshape(1, 2 * D_MODEL))


def _conv_in_kernel(x_ref, g_ref, w_ref, b_ref, o_ref):
    xn = _rms(x_ref[...], g_ref[...]).astype(BF16)
    h = jnp.dot(xn, w_ref[...], preferred_element_type=F32) + b_ref[...]
    o_ref[...] = (h[:, :D_MODEL] * jax.nn.sigmoid(h[:, D_MODEL:])).astype(BF16)


def _conv_in(x2, g, w, b):
    n_tok = x2.shape[0]
    tt = _token_tile(n_tok)
    return pl.pallas_call(
        _conv_in_kernel,
        grid=(n_tok // tt,),
        in_specs=[
            pl.BlockSpec((tt, D_MODEL), lambda t: (t, 0)),
            _const_spec((1, D_MODEL)),
            _const_spec((D_MODEL, 2 * D_MODEL)),
            _const_spec((1, 2 * D_MODEL)),
        ],
        out_specs=pl.BlockSpec((tt, D_MODEL), lambda t: (t, 0)),
        out_shape=jax.ShapeDtypeStruct((n_tok, D_MODEL), BF16),
        compiler_params=_cparams(1),
        name="conv_in",
    )(x2, g.reshape(1, D_MODEL), w, b.reshape(1, 2 * D_MODEL))


def _conv_out_kernel(u_ref, up_ref, un_ref, x_ref, wdw_ref, bdw_ref, lg_ref, lb_ref, w_ref, b_ref,
                     o_ref, buf_ref, cv_ref, *, tt, rb):
    t = pl.program_id(1)
    last = pl.num_programs(1) - 1
    prev = jnp.where(t > 0, up_ref[0].astype(F32), 0.0)
    cur = u_ref[0].astype(F32)
    nxt = jnp.where(t < last, un_ref[0].astype(F32), 0.0)
    for s in range(N_SLABS):
        sl = slice(s * LANES, (s + 1) * LANES)
        buf_ref[s, 0:HALO, :] = prev[:, sl]
        buf_ref[s, HALO:HALO + tt, :] = cur[:, sl]
        buf_ref[s, HALO + tt:, :] = nxt[:, sl]
    off = HALO - CONV_PAD
    for s in range(N_SLABS):
        sl = slice(s * LANES, (s + 1) * LANES)
        for r0 in range(0, tt, rb):
            acc = jnp.broadcast_to(bdw_ref[:, sl], (rb, LANES))
            for k in range(CONV_WIDTH):
                acc = acc + wdw_ref[k:k + 1, sl] * buf_ref[s, r0 + off + k:r0 + off + k + rb, :]
            cv_ref[r0:r0 + rb, sl] = acc
    h = cv_ref[...]
    mu = jnp.mean(h, axis=-1, keepdims=True)
    hc = h - mu
    var = jnp.mean(hc * hc, axis=-1, keepdims=True)
    y = hc * lax.rsqrt(var + EPS) * lg_ref[...] + lb_ref[...]
    y = jax.nn.silu(y).astype(BF16)
    o_ref[0] = x_ref[0] + jnp.dot(y, w_ref[...], preferred_element_type=F32) + b_ref[...]


def _conv_out(u, x, w_dw, b_dw, ln_g, ln_b, w_out, b_out):
    bsz, seq, _ = x.shape
    tt = min(256, seq)
    assert seq % tt == 0 and tt % HALO == 0
    rb = min(64, tt)
    n_halo = seq // HALO
    per = tt // HALO
    row = lambda v: v.reshape(1, D_MODEL)
    return pl.pallas_call(
        functools.partial(_conv_out_kernel, tt=tt, rb=rb),
        grid=(bsz, seq // tt),
        in_specs=[
            pl.BlockSpec((1, tt, D_MODEL), lambda b, t: (b, t, 0)),
            pl.BlockSpec((1, HALO, D_MODEL), lambda b, t: (b, jnp.maximum(t * per - 1, 0), 0)),
            pl.BlockSpec((1, HALO, D_MODEL), lambda b, t: (b, jnp.minimum((t + 1) * per, n_halo - 1), 0)),
            pl.BlockSpec((1, tt, D_MODEL), lambda b, t: (b, t, 0)),
            _const_spec((CONV_WIDTH, D_MODEL)),
            _const_spec((1, D_MODEL)),
            _const_spec((1, D_MODEL)),
            _const_spec((1, D_MODEL)),
            _const_spec((D_MODEL, D_MODEL)),
            _const_spec((1, D_MODEL)),
        ],
        out_specs=pl.BlockSpec((1, tt, D_MODEL), lambda b, t: (b, t, 0)),
        out_shape=jax.ShapeDtypeStruct(x.shape, F32),
        scratch_shapes=[pltpu.VMEM((N_SLABS, tt + 2 * HALO, LANES), F32), pltpu.VMEM((tt, D_MODEL), F32)],
        compiler_params=_cparams(2),
        name="conv_out",
    )(u, u, u, x, w_dw, row(b_dw), row(ln_g), row(ln_b), w_out, row(b_out))


def _attn_kernel(x_ref, k_ref, v_ref, g_ref, wq_ref, wo_ref, o_ref):
    x = x_ref[0]
    xn = _rms(x, g_ref[...]).astype(BF16)
    q = jnp.dot(xn, wq_ref[...], preferred_element_type=F32)
    q = (q * (XHEAD_DIM ** -0.5)).astype(BF16)
    heads = []
    for h in range(N_XHEADS):
        sl = slice(h * XHEAD_DIM, (h + 1) * XHEAD_DIM)
        s = lax.dot_general(q[:, sl], k_ref[0, 0, :, sl], (((1,), (1,)), ((), ())),
                            preferred_element_type=F32)
        e = jnp.exp(s - jnp.max(s, axis=-1, keepdims=True))
        p = (e / jnp.sum(e, axis=-1, keepdims=True)).astype(BF16)
        heads.append(jnp.dot(p, v_ref[0, 0, :, sl], preferred_element_type=F32).astype(BF16))
    o = jnp.concatenate(heads, axis=-1)
    o_ref[0] = x + jnp.dot(o, wo_ref[...], preferred_element_type=F32)


def _cross_attention(x, k_all, v_all, layer, g, wq, wo):
    bsz, seq, _ = x.shape
    n_mem = k_all.shape[2]
    tt = _token_tile(seq)
    kv_spec = pl.BlockSpec((1, 1, n_mem, D_MODEL), lambda b, t: (layer, b, 0, 0))
    return pl.pallas_call(
        _attn_kernel,
        grid=(bsz, seq // tt),
        in_specs=[
            pl.BlockSpec((1, tt, D_MODEL), lambda b, t: (b, t, 0)),
            kv_spec, kv_spec,
            _const_spec((1, D_MODEL)),
            _const_spec((D_MODEL, D_MODEL)),
            _const_spec((D_MODEL, D_MODEL)),
        ],
        out_specs=pl.BlockSpec((1, tt, D_MODEL), lambda b, t: (b, t, 0)),
        out_shape=jax.ShapeDtypeStruct(x.shape, F32),
        compiler_params=_cparams(2),
        name="cross_attn",
    )(x, k_all, v_all, g.reshape(1, D_MODEL), wq, wo)


def _mlp_kernel(x_ref, g_ref, win_ref, wout_ref, gf_ref, o_ref, *, final):
    x = x_ref[...]
    xn = _rms(x, g_ref[...]).astype(BF16)
    acc = x
    for c in range(D_FF // D_MODEL):
        sl = slice(c * D_MODEL, (c + 1) * D_MODEL)
        h = jnp.dot(xn, win_ref[:, sl], preferred_element_type=F32)
        h = jnp.square(jnp.maximum(h, 0.0)).astype(BF16)
        acc = acc + jnp.dot(h, wout_ref[sl, :], preferred_element_type=F32)
    o_ref[...] = _rms(acc, gf_ref[...]) if final else acc


def _mlp(x2, g, w_in, w_out, g_final, final):
    n_tok = x2.shape[0]
    tt = _token_tile(n_tok)
    return pl.pallas_call(
        functools.partial(_mlp_kernel, final=final),
        grid=(n_tok // tt,),
        in_specs=[
            pl.BlockSpec((tt, D_MODEL), lambda t: (t, 0)),
            _const_spec((1, D_MODEL)),
            _const_spec((D_MODEL, D_FF)),
            _const_spec((D_FF, D_MODEL)),
            _const_spec((1, D_MODEL)),
        ],
        out_specs=pl.BlockSpec((tt, D_MODEL), lambda t: (t, 0)),
        out_shape=jax.ShapeDtypeStruct((n_tok, D_MODEL), F32),
        compiler_params=_cparams(1),
        name="mlp_final" if final else "mlp",
    )(x2, g.reshape(1, D_MODEL), w_in, w_out, g_final.reshape(1, D_MODEL))


def _trunk(x, mem, p, s5_preps):
    bsz, seq, _ = x.shape
    depth = p['attn_w_q'].shape[0]
    k_all, v_all = _kv_all_layers(mem, p['norm_mem'], p['attn_w_kv'])
    flat = lambda a: a.reshape(bsz * seq, D_MODEL)
    for i in range(depth):
        j = i // 2
        if i % 2 == 0:
            u = _rms_to_slabs(flat(x), p['norm_mix'][i])
            g = _s5_scan(u, seq, s5_preps[j])
            x = _glu_residual(g, flat(x), p['ssm_w_glu'][j], p['ssm_b_glu'][j]).reshape(x.shape)
        else:
            u = _conv_in(flat(x), p['norm_mix'][i], p['conv_w_in'][j], p['conv_b_in'][j])
            x = _conv_out(u.reshape(x.shape), x, p['conv_w_dw'][j], p['conv_b_dw'][j],
                          p['conv_ln_g'][j], p['conv_ln_b'][j], p['conv_w_out'][j], p['conv_b_out'][j])
        x = _cross_attention(x, k_all, v_all, i, p['norm_xq'][i], p['attn_w_q'][i], p['attn_w_o'][i])
        x = _mlp(flat(x), p['norm_ffn'][i], p['mlp_w_in'][i], p['mlp_w_out'][i],
                 p['norm_final'], i == depth - 1).reshape(x.shape)
    return x


def kernel(x_prompt, x_sample, mem_prompt, mem_sample, norm_mix, norm_xq, norm_mem, norm_ffn, norm_final, ssm_a_re, ssm_a_im, ssm_log_step, ssm_b_re, ssm_b_im, ssm_c_re, ssm_c_im, ssm_d, ssm_w_glu, ssm_b_glu, conv_w_in, conv_b_in, conv_w_dw, conv_b_dw, conv_ln_g, conv_ln_b, conv_w_out, conv_b_out, attn_w_q, attn_w_kv, attn_w_o, mlp_w_in, mlp_w_out):
    p = dict(norm_mix=norm_mix, norm_xq=norm_xq, norm_mem=norm_mem, norm_ffn=norm_ffn,
             norm_final=norm_final, ssm_b_glu=ssm_b_glu,
             conv_b_in=conv_b_in, conv_w_dw=conv_w_dw, conv_b_dw=conv_b_dw,
             conv_ln_g=conv_ln_g, conv_ln_b=conv_ln_b, conv_b_out=conv_b_out)
    for name, w in (('ssm_w_glu', ssm_w_glu), ('conv_w_in', conv_w_in), ('conv_w_out', conv_w_out),
                    ('attn_w_q', attn_w_q), ('attn_w_kv', attn_w_kv), ('attn_w_o', attn_w_o),
                    ('mlp_w_in', mlp_w_in), ('mlp_w_out', mlp_w_out)):
        p[name] = w.astype(BF16)
    s5_preps = [_s5_prepare(ssm_a_re[j], ssm_a_im[j], ssm_log_step[j], ssm_b_re[j], ssm_b_im[j],
                            ssm_c_re[j], ssm_c_im[j], ssm_d[j]) for j in range(ssm_a_re.shape[0])]
    y_prompt = _trunk(x_prompt, mem_prompt, p, s5_preps)
    y_sample = _trunk(x_sample, mem_sample, p, s5_preps)
    return (y_prompt, y_sample)
```

```python
import functools
import math

import jax
import jax.numpy as jnp
from jax import lax
from jax.experimental import pallas as pl
from jax.experimental.pallas import tpu as pltpu

D_MODEL = 1024
GROUP_CH = 16
N_GROUPS = D_MODEL // GROUP_CH
STATE = 64
CONV_WIDTH = 31
CONV_PAD = (CONV_WIDTH - 1) // 2
N_XHEADS = 4
XHEAD_DIM = D_MODEL // N_XHEADS
D_FF = 4 * D_MODEL
EPS = 1e-6

LANES = 128
N_SLABS = D_MODEL // LANES
GROUPS_PER_SLAB = LANES // GROUP_CH
S5_CHUNK = 16
N_HALF = S5_CHUNK // GROUPS_PER_SLAB
PAIR = 2
N_PAIRS = GROUPS_PER_SLAB // PAIR
PAIR_COLS = PAIR * S5_CHUNK * GROUP_CH
PAIR_STATE = 4 * PAIR * STATE
N_STATE_BLOCKS = 2 * N_PAIRS
HALO = 16
VMEM_LIMIT = 56 * 1024 * 1024

F32 = jnp.float32
BF16 = jnp.bfloat16


def _cparams(n_axes):
    return pltpu.CompilerParams(
        dimension_semantics=("arbitrary",) * n_axes, vmem_limit_bytes=VMEM_LIMIT)


def _const_spec(shape):
    nd = len(shape)
    return pl.BlockSpec(shape, lambda *_: (0,) * nd, pipeline_mode=pl.Buffered(1))


def _rms(x, g):
    return x * lax.rsqrt(jnp.mean(x * x, axis=-1, keepdims=True) + EPS) * g


def _token_tile(n):
    t = min(512, n)
    assert n % t == 0
    return t


def _kv_kernel(mem_ref, g_ref, w_ref, k_ref, v_ref):
    mn = _rms(mem_ref[0], g_ref[0]).astype(BF16)
    kv = jnp.dot(mn, w_ref[0], preferred_element_type=F32)
    k_ref[0, 0] = kv[:, :D_MODEL].astype(BF16)
    v_ref[0, 0] = kv[:, D_MODEL:].astype(BF16)


def _kv_all_layers(mem, norm_mem, w_kv):
    bsz, n_mem, _ = mem.shape
    depth = w_kv.shape[0]
    out = jax.ShapeDtypeStruct((depth, bsz, n_mem, D_MODEL), BF16)
    return pl.pallas_call(
        _kv_kernel,
        grid=(depth, bsz),
        in_specs=[
            pl.BlockSpec((1, n_mem, D_MODEL), lambda i, b: (b, 0, 0)),
            pl.BlockSpec((1, 1, D_MODEL), lambda i, b: (i, 0, 0)),
            pl.BlockSpec((1, D_MODEL, 2 * D_MODEL), lambda i, b: (i, 0, 0)),
        ],
        out_specs=[
            pl.BlockSpec((1, 1, n_mem, D_MODEL), lambda i, b: (i, b, 0, 0)),
            pl.BlockSpec((1, 1, n_mem, D_MODEL), lambda i, b: (i, b, 0, 0)),
        ],
        out_shape=[out, out],
        compiler_params=_cparams(2),
        name="kv_proj",
    )(mem, norm_mem.reshape(depth, 1, D_MODEL), w_kv)


def _rms_slab_kernel(x_ref, g_ref, o_ref):
    xn = _rms(x_ref[...], g_ref[...]).astype(BF16)
    for s in range(N_SLABS):
        o_ref[s] = xn[:, s * LANES:(s + 1) * LANES]


def _rms_to_slabs(x2, g):
    n_tok = x2.shape[0]
    tt = _token_tile(n_tok)
    return pl.pallas_call(
        _rms_slab_kernel,
        grid=(n_tok // tt,),
        in_specs=[pl.BlockSpec((tt, D_MODEL), lambda t: (t, 0)), _const_spec((1, D_MODEL))],
        out_specs=pl.BlockSpec((N_SLABS, tt, LANES), lambda t: (0, t, 0)),
        out_shape=jax.ShapeDtypeStruct((N_SLABS, n_tok, LANES), BF16),
        compiler_params=_cparams(1),
        name="rms_slabs",
    )(x2, g.reshape(1, D_MODEL))


def _lane_block_transpose(vs):
    vs = list(vs)
    blk = lax.broadcasted_iota(jnp.int32, vs[0].shape, 1) // GROUP_CH
    d = 1
    while d < len(vs):
        upper = (blk & d) != 0
        for i in range(len(vs)):
            if i & d:
                continue
            a, b = vs[i], vs[i | d]
            vs[i] = jnp.where(upper, pltpu.roll(b, d * GROUP_CH, 1), a)
            vs[i | d] = jnp.where(upper, b, pltpu.roll(a, LANES - d * GROUP_CH, 1))
        d *= 2
    return vs


def _s5_kernel(u_ref, win_ref, wi_ref, wo_ref, lam_ref, d_ref, o_ref,
               ulhs_ref, g_ref, stage_ref, zs_ref, sf_ref, sb_ref, *, bb, nc, rt, rs, zp, pitch):
    tiles = [(b * nc + k0, b * pitch + k0) for b in range(bb) for k0 in range(0, nc, rt)]
    as_words = lambda v: pltpu.bitcast(v, jnp.int32)
    as_bf16 = lambda v: pltpu.bitcast(v, BF16)

    for r0, p0 in tiles:
        for q0 in range(0, rt, rs):
            tok0 = (r0 + q0) * S5_CHUNK
            stage_ref[...] = u_ref[0, tok0:tok0 + rs * S5_CHUNK, :].astype(F32)
            steps = [as_words(stage_ref[pl.ds(t, rs, stride=S5_CHUNK), :].astype(BF16))
                     for t in range(S5_CHUNK)]
            runs = [_lane_block_transpose(steps[h * GROUPS_PER_SLAB:(h + 1) * GROUPS_PER_SLAB])
                    for h in range(N_HALF)]
            for pp in range(N_PAIRS):
                ulhs_ref[pp, r0 + q0:r0 + q0 + rs, :] = jnp.concatenate(
                    [as_bf16(runs[h][PAIR * pp + g]) for g in range(PAIR) for h in range(N_HALF)], axis=-1)
        for pp in range(N_PAIRS):
            s = jnp.dot(ulhs_ref[pp, r0:r0 + rt, :], win_ref[0, pp], preferred_element_type=F32)
            for c in range(2):
                sf_ref[2 * pp + c, p0:p0 + rt, :] = s[:, c * LANES:(c + 1) * LANES]
                sb_ref[2 * pp + c, p0:p0 + rt, :] = s[:, (2 + c) * LANES:(3 + c) * LANES]

    lam = lam_ref[0]

    def rows_of_chunk(k):
        return pl.ds(k, bb, stride=pitch) if bb > 1 else pl.ds(k, 1)

    def advance(s_ref, lam_row, idx, carry):
        new = [None] * N_STATE_BLOCKS
        for pp in range(N_PAIRS):
            re, im = 2 * pp, 2 * pp + 1
            l_re = lam_row[:, re * LANES:(re + 1) * LANES]
            l_im = lam_row[:, im * LANES:(im + 1) * LANES]
            s_re, s_im = s_ref[re, idx, :], s_ref[im, idx, :]
            c_re, c_im = carry[re], carry[im]
            s_ref[re, idx, :] = c_re
            s_ref[im, idx, :] = c_im
            new[re] = l_re * c_re - l_im * c_im + s_re
            new[im] = l_re * c_im + l_im * c_re + s_im
        return tuple(new)

    def body(k, carry):
        cf, cb = carry
        return (advance(sf_ref, lam[0:1], rows_of_chunk(k), cf),
                advance(sb_ref, lam[1:2], rows_of_chunk(nc - 1 - k), cb))

    zero = tuple(jnp.zeros((bb, LANES), F32) for _ in range(N_STATE_BLOCKS))
    lax.fori_loop(0, nc, body, (zero, zero))

    for r0, p0 in tiles:
        for pp in range(N_PAIRS):
            u = ulhs_ref[pp, r0:r0 + rt, :]
            entering = jnp.concatenate([s_ref[2 * pp + c, p0:p0 + rt, :]
                                        for s_ref in (sf_ref, sb_ref) for c in range(2)], axis=-1).astype(BF16)
            y = jnp.dot(u, wi_ref[0, pp], preferred_element_type=F32)
            y = y + jnp.dot(entering, wo_ref[0, pp], preferred_element_type=F32)
            g_ref[pp, 0:rt, :] = jax.nn.gelu(y + d_ref[0, pp] * u.astype(F32)).astype(BF16)
        for q0 in range(0, rt, rs):
            for h in range(N_HALF):
                run = _lane_block_transpose([
                    as_words(g_ref[q // PAIR, q0:q0 + rs,
                                   ((q % PAIR) * N_HALF + h) * LANES:((q % PAIR) * N_HALF + h + 1) * LANES])
                    for q in range(GROUPS_PER_SLAB)])
                for i in range(GROUPS_PER_SLAB):
                    z0 = (h * GROUPS_PER_SLAB + i) * zp
                    zs_ref[z0:z0 + rs, :] = as_bf16(run[i]).astype(F32)
            tok0 = (r0 + q0) * S5_CHUNK
            o_ref[0, tok0:tok0 + rs * S5_CHUNK, :] = jnp.concatenate(
                [zs_ref[pl.ds(k, S5_CHUNK, stride=zp), :] for k in range(rs)], axis=0).astype(BF16)


def _s5_prepare(a_re, a_im, log_step, b_re, b_im, c_re, c_im, d_skip):
    t_len = S5_CHUNK
    step = jnp.exp(log_step.astype(F32))[..., None]
    ar, ai = a_re.astype(F32) * step, a_im.astype(F32) * step
    n = jnp.arange(t_len + 1, dtype=F32)[None, :, None, None]
    mag = jnp.exp(ar[:, None] * n)
    pw_re, pw_im = mag * jnp.cos(ai[:, None] * n), mag * jnp.sin(ai[:, None] * n)
    num_re, num_im = pw_re[:, 1] - 1.0, pw_im[:, 1]
    den = a_re * a_re + a_im * a_im
    z_re = (num_re * a_re + num_im * a_im) / den
    z_im = (num_im * a_re - num_re * a_im) / den
    bb_re = z_re[..., None] * b_re - z_im[..., None] * b_im
    bb_im = z_re[..., None] * b_im + z_im[..., None] * b_re

    def lam_pow_b(d, idx):
        pr, pi = pw_re[d][idx], pw_im[d][idx]
        return (pr[..., None] * bb_re[d][None] - pi[..., None] * bb_im[d][None],
                pr[..., None] * bb_im[d][None] + pi[..., None] * bb_re[d][None])

    def c_lam_pow(d, idx):
        pr, pi = pw_re[d][idx][:, :, None, :], pw_im[d][idx][:, :, None, :]
        return (c_re[d][None] * pr - c_im[d][None] * pi, c_re[d][None] * pi + c_im[d][None] * pr)

    t_idx = jnp.arange(t_len)

    n_pairs = N_GROUPS // PAIR
    tc = t_len * GROUP_CH
    eye = jnp.eye(PAIR, dtype=F32)

    def pair_operator(x):
        return x.reshape(N_SLABS, N_PAIRS, PAIR_COLS, PAIR_COLS).astype(BF16)

    x = jnp.stack(lam_pow_b(0, t_len - 1 - t_idx) + lam_pow_b(1, t_idx), 0)
    x = x.transpose(2, 1, 4, 0, 3).reshape(n_pairs, PAIR, tc, 4, 1, STATE)
    w_in = pair_operator(x * eye[None, :, None, None, :, None])

    (mf_re, mf_im), (mb_re, mb_im) = c_lam_pow(0, t_idx + 1), c_lam_pow(1, t_len - t_idx)
    x = jnp.stack([mf_re, -mf_im, mb_re, -mb_im], 0)
    x = x.transpose(2, 0, 4, 1, 3).reshape(n_pairs, PAIR, 4, STATE, 1, tc)
    w_out = pair_operator((x * eye[None, :, None, None, :, None]).transpose(0, 2, 1, 3, 4, 5))

    def impulse(d):
        lr, li = lam_pow_b(d, t_idx)
        return (jnp.einsum('gcp,ngpk->ngck', c_re[d], lr)
                - jnp.einsum('gcp,ngpk->ngck', c_im[d], li))

    k_f, k_b = impulse(0), impulse(1)
    delta = t_idx[None, :] - t_idx[:, None]
    fwd = jnp.where((delta >= 0)[..., None, None, None], k_f[jnp.clip(delta, 0)], 0.0)
    bwd = jnp.where((delta <= 0)[..., None, None, None], k_b[jnp.clip(-delta, 0)], 0.0)
    x = (fwd + bwd).transpose(2, 0, 4, 1, 3).reshape(n_pairs, PAIR, tc, 1, tc)
    w_intra = pair_operator(x * eye[None, :, None, :, None])

    def pair_state(x_re, x_im):
        blocks = [v.reshape(N_SLABS, N_PAIRS, 1, PAIR * STATE) for v in (x_re, x_im)]
        return jnp.concatenate(blocks, axis=2).reshape(N_SLABS, N_STATE_BLOCKS * LANES)

    lam_t = jnp.stack([pair_state(pw_re[0, t_len], pw_im[0, t_len]),
                       pair_state(pw_re[1, t_len], pw_im[1, t_len])], axis=1)
    d = jnp.broadcast_to(d_skip.astype(F32).reshape(N_SLABS, N_PAIRS, PAIR, 1, GROUP_CH),
                         (N_SLABS, N_PAIRS, PAIR, t_len, GROUP_CH)).reshape(N_SLABS, N_PAIRS, 1, PAIR_COLS)
    return w_in, w_intra, w_out, lam_t, d


def _s5_scan(u_slabs, seq, prep):
    w_in, w_intra, w_out, lam_t, d = prep
    n_tok = u_slabs.shape[1]
    nc = seq // S5_CHUNK
    n_seq = n_tok // seq
    bb = max(1, min(n_seq, 1024 // nc))
    assert n_seq % bb == 0 and seq % S5_CHUNK == 0
    rows = bb * nc
    rt = min(128, nc)
    rs = min(64, rt)
    assert nc % rt == 0 and rt % rs == 0
    zp = rs + 8
    pitch = nc + 4 if nc % 8 == 0 else nc
    pair_w = pl.BlockSpec((1, N_PAIRS, PAIR_COLS, PAIR_COLS), lambda s, r: (s, 0, 0, 0))
    tok_spec = pl.BlockSpec((1, rows * S5_CHUNK, LANES), lambda s, r: (s, r, 0))
    state = pltpu.VMEM((N_STATE_BLOCKS, bb * pitch, LANES), F32)
    return pl.pallas_call(
        functools.partial(_s5_kernel, bb=bb, nc=nc, rt=rt, rs=rs, zp=zp, pitch=pitch),
        grid=(N_SLABS, n_seq // bb),
        in_specs=[
            tok_spec, pair_w, pair_w, pair_w,
            pl.BlockSpec((1, 2, N_STATE_BLOCKS * LANES), lambda s, r: (s, 0, 0)),
            pl.BlockSpec((1, N_PAIRS, 1, PAIR_COLS), lambda s, r: (s, 0, 0, 0)),
        ],
        out_specs=tok_spec,
        out_shape=jax.ShapeDtypeStruct(u_slabs.shape, BF16),
        scratch_shapes=[pltpu.VMEM((N_PAIRS, rows, PAIR_COLS), BF16),
                        pltpu.VMEM((N_PAIRS, rt, PAIR_COLS), BF16),
                        pltpu.VMEM((rs * S5_CHUNK, LANES), F32),
                        pltpu.VMEM((S5_CHUNK * zp, LANES), F32), state, state],
        compiler_params=_cparams(2),
        name="s5_scan",
    )(u_slabs, w_in, w_intra, w_out, lam_t, d)


def _glu_kernel(g_ref, x_ref, w_ref, b_ref, o_ref):
    g = jnp.concatenate([g_ref[s] for s in range(N_SLABS)], axis=-1)
    h = jnp.dot(g, w_ref[...], preferred_element_type=F32) + b_ref[...]
    o_ref[...] = x_ref[...] + h[:, :D_MODEL] * jax.nn.sigmoid(h[:, D_MODEL:])


def _glu_residual(g_slabs, x2, w, b):
    n_tok = x2.shape[0]
    tt = _token_tile(n_tok)
    return pl.pallas_call(
        _glu_kernel,
        grid=(n_tok // tt,),
        in_specs=[
            pl.BlockSpec((N_SLABS, tt, LANES), lambda t: (0, t, 0)),
            pl.BlockSpec((tt, D_MODEL), lambda t: (t, 0)),
            _const_spec((D_MODEL, 2 * D_MODEL)),
            _const_spec((1, 2 * D_MODEL)),
        ],
        out_specs=pl.BlockSpec((tt, D_MODEL), lambda t: (t, 0)),
        out_shape=jax.ShapeDtypeStruct((n_tok, D_MODEL), F32),
        compiler_params=_cparams(1),
        name="s5_glu",
    )(g_slabs, x2, w, b.reshape(1, 2 * D_MODEL))


def _conv_in_kernel(x_ref, g_ref, w_ref, b_ref, o_ref):
    xn = _rms(x_ref[...], g_ref[...]).astype(BF16)
    h = jnp.dot(xn, w_ref[...], preferred_element_type=F32) + b_ref[...]
    o_ref[...] = (h[:, :D_MODEL] * jax.nn.sigmoid(h[:, D_MODEL:])).astype(BF16)


def _conv_in(x2, g, w, b):
    n_tok = x2.shape[0]
    tt = _token_tile(n_tok)
    return pl.pallas_call(
        _conv_in_kernel,
        grid=(n_tok // tt,),
        in_specs=[
            pl.BlockSpec((tt, D_MODEL), lambda t: (t, 0)),
            _const_spec((1, D_MODEL)),
            _const_spec((D_MODEL, 2 * D_MODEL)),
            _const_spec((1, 2 * D_MODEL)),
        ],
        out_specs=pl.BlockSpec((tt, D_MODEL), lambda t: (t, 0)),
        out_shape=jax.ShapeDtypeStruct((n_tok, D_MODEL), BF16),
        compiler_params=_cparams(1),
        name="conv_in",
    )(x2, g.reshape(1, D_MODEL), w, b.reshape(1, 2 * D_MODEL))


def _conv_out_kernel(u_ref, up_ref, un_ref, x_ref, wdw_ref, bdw_ref, lg_ref, lb_ref, w_ref, b_ref,
                     o_ref, buf_ref, cv_ref, *, tt, rb):
    t = pl.program_id(1)
    last = pl.num_programs(1) - 1
    prev = jnp.where(t > 0, up_ref[0].astype(F32), 0.0)
    cur = u_ref[0].astype(F32)
    nxt = jnp.where(t < last, un_ref[0].astype(F32), 0.0)
    for s in range(N_SLABS):
        sl = slice(s * LANES, (s + 1) * LANES)
        buf_ref[s, 0:HALO, :] = prev[:, sl]
        buf_ref[s, HALO:HALO + tt, :] = cur[:, sl]
        buf_ref[s, HALO + tt:, :] = nxt[:, sl]
    off = HALO - CONV_PAD
    for s in range(N_SLABS):
        sl = slice(s * LANES, (s + 1) * LANES)
        for r0 in range(0, tt, rb):
            acc = jnp.broadcast_to(bdw_ref[:, sl], (rb, LANES))
            for k in range(CONV_WIDTH):
                acc = acc + wdw_ref[k:k + 1, sl] * buf_ref[s, r0 + off + k:r0 + off + k + rb, :]
            cv_ref[r0:r0 + rb, sl] = acc
    h = cv_ref[...]
    mu = jnp.mean(h, axis=-1, keepdims=True)
    hc = h - mu
    var = jnp.mean(hc * hc, axis=-1, keepdims=True)
    y = hc * lax.rsqrt(var + EPS) * lg_ref[...] + lb_ref[...]
    y = jax.nn.silu(y).astype(BF16)
    o_ref[0] = x_ref[0] + jnp.dot(y, w_ref[...], preferred_element_type=F32) + b_ref[...]


def _conv_out(u, x, w_dw, b_dw, ln_g, ln_b, w_out, b_out):
    bsz, seq, _ = x.shape
    tt = min(256, seq)
    assert seq % tt == 0 and tt % HALO == 0
    rb = min(64, tt)
    n_halo = seq // HALO
    per = tt // HALO
    row = lambda v: v.reshape(1, D_MODEL)
    return pl.pallas_call(
        functools.partial(_conv_out_kernel, tt=tt, rb=rb),
        grid=(bsz, seq // tt),
        in_specs=[
            pl.BlockSpec((1, tt, D_MODEL), lambda b, t: (b, t, 0)),
            pl.BlockSpec((1, HALO, D_MODEL), lambda b, t: (b, jnp.maximum(t * per - 1, 0), 0)),
            pl.BlockSpec((1, HALO, D_MODEL), lambda b, t: (b, jnp.minimum((t + 1) * per, n_halo - 1), 0)),
            pl.BlockSpec((1, tt, D_MODEL), lambda b, t: (b, t, 0)),
            _const_spec((CONV_WIDTH, D_MODEL)),
            _const_spec((1, D_MODEL)),
            _const_spec((1, D_MODEL)),
            _const_spec((1, D_MODEL)),
            _const_spec((D_MODEL, D_MODEL)),
            _const_spec((1, D_MODEL)),
        ],
        out_specs=pl.BlockSpec((1, tt, D_MODEL), lambda b, t: (b, t, 0)),
        out_shape=jax.ShapeDtypeStruct(x.shape, F32),
        scratch_shapes=[pltpu.VMEM((N_SLABS, tt + 2 * HALO, LANES), F32), pltpu.VMEM((tt, D_MODEL), F32)],
        compiler_params=_cparams(2),
        name="conv_out",
    )(u, u, u, x, w_dw, row(b_dw), row(ln_g), row(ln_b), w_out, row(b_out))


def _attn_kernel(x_ref, k_ref, v_ref, g_ref, wq_ref, wo_ref, o_ref):
    x = x_ref[0]
    xn = _rms(x, g_ref[...]).astype(BF16)
    q = jnp.dot(xn, wq_ref[...], preferred_element_type=F32)
    q = (q * (XHEAD_DIM ** -0.5)).astype(BF16)
    heads = []
    for h in range(N_XHEADS):
        sl = slice(h * XHEAD_DIM, (h + 1) * XHEAD_DIM)
        s = lax.dot_general(q[:, sl], k_ref[0, 0, :, sl], (((1,), (1,)), ((), ())),
                            preferred_element_type=F32)
        e = jnp.exp(s - jnp.max(s, axis=-1, keepdims=True))
        p = (e / jnp.sum(e, axis=-1, keepdims=True)).astype(BF16)
        heads.append(jnp.dot(p, v_ref[0, 0, :, sl], preferred_element_type=F32).astype(BF16))
    o = jnp.concatenate(heads, axis=-1)
    o_ref[0] = x + jnp.dot(o, wo_ref[...], preferred_element_type=F32)


def _cross_attention(x, k_all, v_all, layer, g, wq, wo):
    bsz, seq, _ = x.shape
    n_mem = k_all.shape[2]
    tt = _token_tile(seq)
    kv_spec = pl.BlockSpec((1, 1, n_mem, D_MODEL), lambda b, t: (layer, b, 0, 0))
    return pl.pallas_call(
        _attn_kernel,
        grid=(bsz, seq // tt),
        in_specs=[
            pl.BlockSpec((1, tt, D_MODEL), lambda b, t: (b, t, 0)),
            kv_spec, kv_spec,
            _const_spec((1, D_MODEL)),
            _const_spec((D_MODEL, D_MODEL)),
            _const_spec((D_MODEL, D_MODEL)),
        ],
        out_specs=pl.BlockSpec((1, tt, D_MODEL), lambda b, t: (b, t, 0)),
        out_shape=jax.ShapeDtypeStruct(x.shape, F32),
        compiler_params=_cparams(2),
        name="cross_attn",
    )(x, k_all, v_all, g.reshape(1, D_MODEL), wq, wo)


def _mlp_kernel(x_ref, g_ref, win_ref, wout_ref, gf_ref, o_ref, *, final):
    x = x_ref[...]
    xn = _rms(x, g_ref[...]).astype(BF16)
    acc = x
    for c in range(D_FF // D_MODEL):
        sl = slice(c * D_MODEL, (c + 1) * D_MODEL)
        h = jnp.dot(xn, win_ref[:, sl], preferred_element_type=F32)
        h = jnp.square(jnp.maximum(h, 0.0)).astype(BF16)
        acc = acc + jnp.dot(h, wout_ref[sl, :], preferred_element_type=F32)
    o_ref[...] = _rms(acc, gf_ref[...]) if final else acc


def _mlp(x2, g, w_in, w_out, g_final, final):
    n_tok = x2.shape[0]
    tt = _token_tile(n_tok)
    return pl.pallas_call(
        functools.partial(_mlp_kernel, final=final),
        grid=(n_tok // tt,),
        in_specs=[
            pl.BlockSpec((tt, D_MODEL), lambda t: (t, 0)),
            _const_spec((1, D_MODEL)),
            _const_spec((D_MODEL, D_FF)),
            _const_spec((D_FF, D_MODEL)),
            _const_spec((1, D_MODEL)),
        ],
        out_specs=pl.BlockSpec((tt, D_MODEL), lambda t: (t, 0)),
        out_shape=jax.ShapeDtypeStruct((n_tok, D_MODEL), F32),
        compiler_params=_cparams(1),
        name="mlp_final" if final else "mlp",
    )(x2, g.reshape(1, D_MODEL), w_in, w_out, g_final.reshape(1, D_MODEL))


def _trunk(x, mem, p, s5_preps):
    bsz, seq, _ = x.shape
    depth = p['attn_w_q'].shape[0]
    k_all, v_all = _kv_all_layers(mem, p['norm_mem'], p['attn_w_kv'])
    flat = lambda a: a.reshape(bsz * seq, D_MODEL)
    for i in range(depth):
        j = i // 2
        if i % 2 == 0:
            u = _rms_to_slabs(flat(x), p['norm_mix'][i])
            g = _s5_scan(u, seq, s5_preps[j])
            x = _glu_residual(g, flat(x), p['ssm_w_glu'][j], p['ssm_b_glu'][j]).reshape(x.shape)
        else:
            u = _conv_in(flat(x), p['norm_mix'][i], p['conv_w_in'][j], p['conv_b_in'][j])
            x = _conv_out(u.reshape(x.shape), x, p['conv_w_dw'][j], p['conv_b_dw'][j],
                          p['conv_ln_g'][j], p['conv_ln_b'][j], p['conv_w_out'][j], p['conv_b_out'][j])
        x = _cross_attention(x, k_all, v_all, i, p['norm_xq'][i], p['attn_w_q'][i], p['attn_w_o'][i])
        x = _mlp(flat(x), p['norm_ffn'][i], p['mlp_w_in'][i], p['mlp_w_out'][i],
                 p['norm_final'], i == depth - 1).reshape(x.shape)
    return x


def kernel(x_prompt, x_sample, mem_prompt, mem_sample, norm_mix, norm_xq, norm_mem, norm_ffn, norm_final, ssm_a_re, ssm_a_im, ssm_log_step, ssm_b_re, ssm_b_im, ssm_c_re, ssm_c_im, ssm_d, ssm_w_glu, ssm_b_glu, conv_w_in, conv_b_in, conv_w_dw, conv_b_dw, conv_ln_g, conv_ln_b, conv_w_out, conv_b_out, attn_w_q, attn_w_kv, attn_w_o, mlp_w_in, mlp_w_out):
    p = dict(norm_mix=norm_mix, norm_xq=norm_xq, norm_mem=norm_mem, norm_ffn=norm_ffn,
             norm_final=norm_final, ssm_b_glu=ssm_b_glu,
             conv_b_in=conv_b_in, conv_w_dw=conv_w_dw, conv_b_dw=conv_b_dw,
             conv_ln_g=conv_ln_g, conv_ln_b=conv_ln_b, conv_b_out=conv_b_out)
    for name, w in (('ssm_w_glu', ssm_w_glu), ('conv_w_in', conv_w_in), ('conv_w_out', conv_w_out),
                    ('attn_w_q', attn_w_q), ('attn_w_kv', attn_w_kv), ('attn_w_o', attn_w_o),
                    ('mlp_w_in', mlp_w_in), ('mlp_w_out', mlp_w_out)):
        p[name] = w.astype(BF16)
    s5_preps = [_s5_prepare(ssm_a_re[j], ssm_a_im[j], ssm_log_step[j], ssm_b_re[j], ssm_b_im[j],
                            ssm_c_re[j], ssm_c_im[j], ssm_d[j]) for j in range(ssm_a_re.shape[0])]
    y_prompt = _trunk(x_prompt, mem_prompt, p, s5_preps)
    y_sample = _trunk(x_sample, mem_sample, p, s5_preps)
    return (y_prompt, y_sample)
```

```python
import functools
import math

import jax
import jax.numpy as jnp
from jax import lax
from jax.experimental import pallas as pl
from jax.experimental.pallas import tpu as pltpu

D_MODEL = 1024
GROUP_CH = 16
N_GROUPS = D_MODEL // GROUP_CH
STATE = 64
CONV_WIDTH = 31
CONV_PAD = (CONV_WIDTH - 1) // 2
N_XHEADS = 4
XHEAD_DIM = D_MODEL // N_XHEADS
D_FF = 4 * D_MODEL
EPS = 1e-6

LANES = 128
N_SLABS = D_MODEL // LANES
GROUPS_PER_SLAB = LANES // GROUP_CH
S5_CHUNK = 16
N_HALF = S5_CHUNK // GROUPS_PER_SLAB
PAIR = 2
N_PAIRS = GROUPS_PER_SLAB // PAIR
PAIR_COLS = PAIR * S5_CHUNK * GROUP_CH
PAIR_STATE = 4 * PAIR * STATE
N_STATE_BLOCKS = 2 * N_PAIRS
HALO = 16
VMEM_LIMIT = 56 * 1024 * 1024

F32 = jnp.float32
BF16 = jnp.bfloat16


def _cparams(n_axes):
    return pltpu.CompilerParams(
        dimension_semantics=("arbitrary",) * n_axes, vmem_limit_bytes=VMEM_LIMIT)


def _const_spec(shape):
    nd = len(shape)
    return pl.BlockSpec(shape, lambda *_: (0,) * nd, pipeline_mode=pl.Buffered(1))


def _rms(x, g):
    return x * lax.rsqrt(jnp.mean(x * x, axis=-1, keepdims=True) + EPS) * g


def _token_tile(n):
    t = min(512, n)
    assert n % t == 0
    return t


def _row_parts(n):
    parts = 2 if n % 32 == 0 else 1
    return [slice(i * n // parts, (i + 1) * n // parts) for i in range(parts)]


def _kv_kernel(mem_ref, g_ref, w_ref, k_ref, v_ref):
    mn = _rms(mem_ref[0], g_ref[0]).astype(BF16)
    kv = jnp.dot(mn, w_ref[0], preferred_element_type=F32)
    k_ref[0, 0] = kv[:, :D_MODEL].astype(BF16)
    v_ref[0, 0] = kv[:, D_MODEL:].astype(BF16)


def _kv_all_layers(mem, norm_mem, w_kv):
    bsz, n_mem, _ = mem.shape
    depth = w_kv.shape[0]
    out = jax.ShapeDtypeStruct((depth, bsz, n_mem, D_MODEL), BF16)
    return pl.pallas_call(
        _kv_kernel,
        grid=(depth, bsz),
        in_specs=[
            pl.BlockSpec((1, n_mem, D_MODEL), lambda i, b: (b, 0, 0)),
            pl.BlockSpec((1, 1, D_MODEL), lambda i, b: (i, 0, 0)),
            pl.BlockSpec((1, D_MODEL, 2 * D_MODEL), lambda i, b: (i, 0, 0)),
        ],
        out_specs=[
            pl.BlockSpec((1, 1, n_mem, D_MODEL), lambda i, b: (i, b, 0, 0)),
            pl.BlockSpec((1, 1, n_mem, D_MODEL), lambda i, b: (i, b, 0, 0)),
        ],
        out_shape=[out, out],
        compiler_params=_cparams(2),
        name="kv_proj",
    )(mem, norm_mem.reshape(depth, 1, D_MODEL), w_kv)


def _store_rms_slabs(x, g, o_ref):
    xn = _rms(x, g).astype(BF16)
    for s in range(N_SLABS):
        o_ref[s] = xn[:, s * LANES:(s + 1) * LANES]


def _rms_slab_kernel(x_ref, g_ref, o_ref):
    _store_rms_slabs(x_ref[...], g_ref[...], o_ref)


def _rms_to_slabs(x2, g):
    n_tok = x2.shape[0]
    tt = _token_tile(n_tok)
    return pl.pallas_call(
        _rms_slab_kernel,
        grid=(n_tok // tt,),
        in_specs=[pl.BlockSpec((tt, D_MODEL), lambda t: (t, 0)), _const_spec((1, D_MODEL))],
        out_specs=pl.BlockSpec((N_SLABS, tt, LANES), lambda t: (0, t, 0)),
        out_shape=jax.ShapeDtypeStruct((N_SLABS, n_tok, LANES), BF16),
        compiler_params=_cparams(1),
        name="rms_slabs",
    )(x2, g.reshape(1, D_MODEL))


def _lane_block_transpose(vs):
    vs = list(vs)
    blk = lax.broadcasted_iota(jnp.int32, vs[0].shape, 1) // GROUP_CH
    d = 1
    while d < len(vs):
        upper = (blk & d) != 0
        for i in range(len(vs)):
            if i & d:
                continue
            a, b = vs[i], vs[i | d]
            vs[i] = jnp.where(upper, pltpu.roll(b, d * GROUP_CH, 1), a)
            vs[i | d] = jnp.where(upper, b, pltpu.roll(a, LANES - d * GROUP_CH, 1))
        d *= 2
    return vs


def _s5_kernel(u_ref, win_ref, wi_ref, wo_ref, lam_ref, d_ref, o_ref,
               ulhs_ref, g_ref, stage_ref, zs_ref, sf_ref, sb_ref, *, bb, nc, rt, rs, zp, pitch):
    tiles = [(b * nc + k0, b * pitch + k0) for b in range(bb) for k0 in range(0, nc, rt)]
    as_words = lambda v: pltpu.bitcast(v, jnp.int32)
    as_bf16 = lambda v: pltpu.bitcast(v, BF16)

    for r0, p0 in tiles:
        for q0 in range(0, rt, rs):
            tok0 = (r0 + q0) * S5_CHUNK
            stage_ref[...] = u_ref[0, tok0:tok0 + rs * S5_CHUNK, :].astype(F32)
            steps = [as_words(stage_ref[pl.ds(t, rs, stride=S5_CHUNK), :].astype(BF16))
                     for t in range(S5_CHUNK)]
            runs = [_lane_block_transpose(steps[h * GROUPS_PER_SLAB:(h + 1) * GROUPS_PER_SLAB])
                    for h in range(N_HALF)]
            for pp in range(N_PAIRS):
                ulhs_ref[pp, r0 + q0:r0 + q0 + rs, :] = jnp.concatenate(
                    [as_bf16(runs[h][PAIR * pp + g]) for g in range(PAIR) for h in range(N_HALF)], axis=-1)
        for pp in range(N_PAIRS):
            s = jnp.dot(ulhs_ref[pp, r0:r0 + rt, :], win_ref[0, pp], preferred_element_type=F32)
            for c in range(2):
                sf_ref[2 * pp + c, p0:p0 + rt, :] = s[:, c * LANES:(c + 1) * LANES]
                sb_ref[2 * pp + c, p0:p0 + rt, :] = s[:, (2 + c) * LANES:(3 + c) * LANES]

    lam = lam_ref[0]

    def rows_of_chunk(k):
        return pl.ds(k, bb, stride=pitch) if bb > 1 else pl.ds(k, 1)

    def advance(s_ref, lam_row, idx, carry):
        new = [None] * N_STATE_BLOCKS
        for pp in range(N_PAIRS):
            re, im = 2 * pp, 2 * pp + 1
            l_re = lam_row[:, re * LANES:(re + 1) * LANES]
            l_im = lam_row[:, im * LANES:(im + 1) * LANES]
            s_re, s_im = s_ref[re, idx, :], s_ref[im, idx, :]
            c_re, c_im = carry[re], carry[im]
            s_ref[re, idx, :] = c_re
            s_ref[im, idx, :] = c_im
            new[re] = l_re * c_re - l_im * c_im + s_re
            new[im] = l_re * c_im + l_im * c_re + s_im
        return tuple(new)

    def body(k, carry):
        cf, cb = carry
        return (advance(sf_ref, lam[0:1], rows_of_chunk(k), cf),
                advance(sb_ref, lam[1:2], rows_of_chunk(nc - 1 - k), cb))

    zero = tuple(jnp.zeros((bb, LANES), F32) for _ in range(N_STATE_BLOCKS))
    lax.fori_loop(0, nc, body, (zero, zero))

    for r0, p0 in tiles:
        for pp in range(N_PAIRS):
            u = ulhs_ref[pp, r0:r0 + rt, :]
            entering = jnp.concatenate([s_ref[2 * pp + c, p0:p0 + rt, :]
                                        for s_ref in (sf_ref, sb_ref) for c in range(2)], axis=-1).astype(BF16)
            y = jnp.dot(u, wi_ref[0, pp], preferred_element_type=F32)
            y = y + jnp.dot(entering, wo_ref[0, pp], preferred_element_type=F32)
            g_ref[pp, 0:rt, :] = jax.nn.gelu(y + d_ref[0, pp] * u.astype(F32)).astype(BF16)
        for q0 in range(0, rt, rs):
            for h in range(N_HALF):
                run = _lane_block_transpose([
                    as_words(g_ref[q // PAIR, q0:q0 + rs,
                                   ((q % PAIR) * N_HALF + h) * LANES:((q % PAIR) * N_HALF + h + 1) * LANES])
                    for q in range(GROUPS_PER_SLAB)])
                for i in range(GROUPS_PER_SLAB):
                    z0 = (h * GROUPS_PER_SLAB + i) * zp
                    zs_ref[z0:z0 + rs, :] = as_bf16(run[i]).astype(F32)
            tok0 = (r0 + q0) * S5_CHUNK
            o_ref[0, tok0:tok0 + rs * S5_CHUNK, :] = jnp.concatenate(
                [zs_ref[pl.ds(k, S5_CHUNK, stride=zp), :] for k in range(rs)], axis=0).astype(BF16)


def _s5_prepare(a_re, a_im, log_step, b_re, b_im, c_re, c_im, d_skip):
    t_len = S5_CHUNK
    step = jnp.exp(log_step.astype(F32))[..., None]
    ar, ai = a_re.astype(F32) * step, a_im.astype(F32) * step
    n = jnp.arange(t_len + 1, dtype=F32)[None, :, None, None]
    mag = jnp.exp(ar[:, None] * n)
    pw_re, pw_im = mag * jnp.cos(ai[:, None] * n), mag * jnp.sin(ai[:, None] * n)
    num_re, num_im = pw_re[:, 1] - 1.0, pw_im[:, 1]
    den = a_re * a_re + a_im * a_im
    z_re = (num_re * a_re + num_im * a_im) / den
    z_im = (num_im * a_re - num_re * a_im) / den
    bb_re = z_re[..., None] * b_re - z_im[..., None] * b_im
    bb_im = z_re[..., None] * b_im + z_im[..., None] * b_re

    t_idx = jnp.arange(t_len)
    n_pairs = N_GROUPS // PAIR
    tc = t_len * GROUP_CH
    cat = lambda *xs: jnp.concatenate(xs, axis=-1)

    def powers(d, idx):
        return tuple(v[d][idx].reshape(t_len, n_pairs, LANES).transpose(1, 0, 2) for v in (pw_re, pw_im))

    def b_rows(x):
        return x.reshape(n_pairs, PAIR, STATE, GROUP_CH).transpose(0, 3, 1, 2).reshape(n_pairs, GROUP_CH, LANES)

    def c_rows(x):
        return x.reshape(n_pairs, PAIR, GROUP_CH, STATE).transpose(0, 2, 1, 3).reshape(n_pairs, GROUP_CH, LANES)

    def pair_operator(p1, r1, p2, r2):
        x = p1[:, :, None, :] * r1[:, None, :, :] + p2[:, :, None, :] * r2[:, None, :, :]
        shape = (1, PAIR, 1, 1, PAIR_STATE)
        same = lax.broadcasted_iota(jnp.int32, shape, 1) == (lax.broadcasted_iota(jnp.int32, shape, 4) // STATE) % PAIR
        return jnp.where(same, x[:, None], 0.0).reshape(N_SLABS, N_PAIRS, PAIR_COLS, PAIR_STATE).astype(BF16)

    (pr_f, pi_f), (pr_b, pi_b) = powers(0, t_len - 1 - t_idx), powers(1, t_idx)
    br_f, bi_f, br_b, bi_b = b_rows(bb_re[0]), b_rows(bb_im[0]), b_rows(bb_re[1]), b_rows(bb_im[1])
    w_in = pair_operator(cat(pr_f, pr_f, pr_b, pr_b), cat(br_f, bi_f, br_b, bi_b),
                         cat(-pi_f, pi_f, -pi_b, pi_b), cat(bi_f, br_f, bi_b, br_b))

    (qr_f, qi_f), (qr_b, qi_b) = powers(0, t_idx + 1), powers(1, t_len - t_idx)
    cr_f, ci_f, cr_b, ci_b = c_rows(c_re[0]), c_rows(c_im[0]), c_rows(c_re[1]), c_rows(c_im[1])
    w_out_t = pair_operator(cat(qr_f, -qi_f, qr_b, -qi_b), cat(cr_f, cr_f, cr_b, cr_b),
                            cat(-qi_f, -qr_f, -qi_b, -qr_b), cat(ci_f, ci_f, ci_b, ci_b))

    def impulse(d, lags):
        pr, pi = (v[d][lags].transpose(1, 0, 2) for v in (pw_re, pw_im))
        z = (cat(pr, -pi)[:, :, None, :] * cat(c_re[d], c_re[d])[:, None, :, :]
             + cat(-pi, -pr)[:, :, None, :] * cat(c_im[d], c_im[d])[:, None, :, :])
        b = cat(bb_re[d].transpose(0, 2, 1), bb_im[d].transpose(0, 2, 1))
        return jnp.einsum('gcq,gmq->gcm', b, z.reshape(N_GROUPS, tc, 2 * STATE), precision=lax.Precision.HIGHEST)

    k_f, k_b = impulse(0, t_idx), impulse(1, t_len - 1 - t_idx)
    col = lax.broadcasted_iota(jnp.int32, (1, 1, tc), 2)
    toe = jnp.stack([
        jnp.where(col >= GROUP_CH * t, jnp.roll(k_f, GROUP_CH * t, axis=-1), 0.0)
        + jnp.where(col < GROUP_CH * (t + 1), jnp.roll(k_b, -GROUP_CH * (t_len - 1 - t), axis=-1), 0.0)
        for t in range(t_len)], axis=1)
    eye = jnp.eye(PAIR, dtype=F32)
    w_intra = (toe.reshape(n_pairs, PAIR, tc, 1, tc) * eye[None, :, None, :, None]).reshape(
        N_SLABS, N_PAIRS, PAIR_COLS, PAIR_COLS).astype(BF16)

    def pair_state(x_re, x_im):
        blocks = [v.reshape(N_SLABS, N_PAIRS, 1, PAIR * STATE) for v in (x_re, x_im)]
        return jnp.concatenate(blocks, axis=2).reshape(N_SLABS, N_STATE_BLOCKS * LANES)

    lam_t = jnp.stack([pair_state(pw_re[0, t_len], pw_im[0, t_len]),
                       pair_state(pw_re[1, t_len], pw_im[1, t_len])], axis=1)
    d = jnp.broadcast_to(d_skip.astype(F32).reshape(N_SLABS, N_PAIRS, PAIR, 1, GROUP_CH),
                         (N_SLABS, N_PAIRS, PAIR, t_len, GROUP_CH)).reshape(N_SLABS, N_PAIRS, 1, PAIR_COLS)
    return w_in, w_intra, jnp.swapaxes(w_out_t, 2, 3), lam_t, d


def _s5_scan(u_slabs, seq, prep):
    w_in, w_intra, w_out, lam_t, d = prep
    n_tok = u_slabs.shape[1]
    nc = seq // S5_CHUNK
    n_seq = n_tok // seq
    bb = max(1, min(n_seq, 1024 // nc))
    assert n_seq % bb == 0 and seq % S5_CHUNK == 0
    rows = bb * nc
    rt = min(128, nc)
    rs = min(64, rt)
    assert nc % rt == 0 and rt % rs == 0
    zp = rs + 8
    pitch = nc + 4 if nc % 8 == 0 else nc
    pair_w = pl.BlockSpec((1, N_PAIRS, PAIR_COLS, PAIR_COLS), lambda s, r: (s, 0, 0, 0))
    tok_spec = pl.BlockSpec((1, rows * S5_CHUNK, LANES), lambda s, r: (s, r, 0))
    state = pltpu.VMEM((N_STATE_BLOCKS, bb * pitch, LANES), F32)
    return pl.pallas_call(
        functools.partial(_s5_kernel, bb=bb, nc=nc, rt=rt, rs=rs, zp=zp, pitch=pitch),
        grid=(N_SLABS, n_seq // bb),
        in_specs=[
            tok_spec, pair_w, pair_w, pair_w,
            pl.BlockSpec((1, 2, N_STATE_BLOCKS * LANES), lambda s, r: (s, 0, 0)),
            pl.BlockSpec((1, N_PAIRS, 1, PAIR_COLS), lambda s, r: (s, 0, 0, 0)),
        ],
        out_specs=tok_spec,
        out_shape=jax.ShapeDtypeStruct(u_slabs.shape, BF16),
        scratch_shapes=[pltpu.VMEM((N_PAIRS, rows, PAIR_COLS), BF16),
                        pltpu.VMEM((N_PAIRS, rt, PAIR_COLS), BF16),
                        pltpu.VMEM((rs * S5_CHUNK, LANES), F32),
                        pltpu.VMEM((S5_CHUNK * zp, LANES), F32), state, state],
        compiler_params=_cparams(2),
        name="s5_scan",
    )(u_slabs, w_in, w_intra, w_out, lam_t, d)


def _glu_kernel(g_ref, x_ref, w_ref, b_ref, o_ref):
    for rows in _row_parts(x_ref.shape[0]):
        g = jnp.concatenate([g_ref[s, rows, :] for s in range(N_SLABS)], axis=-1)
        h = jnp.dot(g, w_ref[...], preferred_element_type=F32) + b_ref[...]
        o_ref[rows, :] = x_ref[rows, :] + h[:, :D_MODEL] * jax.nn.sigmoid(h[:, D_MODEL:])


def _glu_residual(g_slabs, x2, w, b):
    n_tok = x2.shape[0]
    tt = _token_tile(n_tok)
    return pl.pallas_call(
        _glu_kernel,
        grid=(n_tok // tt,),
        in_specs=[
            pl.BlockSpec((N_SLABS, tt, LANES), lambda t: (0, t, 0)),
            pl.BlockSpec((tt, D_MODEL), lambda t: (t, 0)),
            _const_spec((D_MODEL, 2 * D_MODEL)),
            _const_spec((1, 2 * D_MODEL)),
        ],
        out_specs=pl.BlockSpec((tt, D_MODEL), lambda t: (t, 0)),
        out_shape=jax.ShapeDtypeStruct((n_tok, D_MODEL), F32),
        compiler_params=_cparams(1),
        name="s5_glu",
    )(g_slabs, x2, w, b.reshape(1, 2 * D_MODEL))


def _conv_in_kernel(x_ref, g_ref, w_ref, b_ref, o_ref):
    for rows in _row_parts(x_ref.shape[0]):
        xn = _rms(x_ref[rows, :], g_ref[...]).astype(BF16)
        h = jnp.dot(xn, w_ref[...], preferred_element_type=F32) + b_ref[...]
        o_ref[rows, :] = (h[:, :D_MODEL] * jax.nn.sigmoid(h[:, D_MODEL:])).astype(BF16)


def _conv_in(x2, g, w, b):
    n_tok = x2.shape[0]
    tt = _token_tile(n_tok)
    return pl.pallas_call(
        _conv_in_kernel,
        grid=(n_tok // tt,),
        in_specs=[
            pl.BlockSpec((tt, D_MODEL), lambda t: (t, 0)),
            _const_spec((1, D_MODEL)),
            _const_spec((D_MODEL, 2 * D_MODEL)),
            _const_spec((1, 2 * D_MODEL)),
        ],
        out_specs=pl.BlockSpec((tt, D_MODEL), lambda t: (t, 0)),
        out_shape=jax.ShapeDtypeStruct((n_tok, D_MODEL), BF16),
        compiler_params=_cparams(1),
        name="conv_in",
    )(x2, g.reshape(1, D_MODEL), w, b.reshape(1, 2 * D_MODEL))


def _conv_out_kernel(u_ref, up_ref, un_ref, x_ref, wdw_ref, bdw_ref, lg_ref, lb_ref, w_ref, b_ref,
                     o_ref, buf_ref, cv_ref, *, tt, rb):
    t = pl.program_id(1)
    last = pl.num_programs(1) - 1
    prev = jnp.where(t > 0, up_ref[0].astype(F32), 0.0)
    cur = u_ref[0].astype(F32)
    nxt = jnp.where(t < last, un_ref[0].astype(F32), 0.0)
    for s in range(N_SLABS):
        sl = slice(s * LANES, (s + 1) * LANES)
        buf_ref[s, 0:HALO, :] = prev[:, sl]
        buf_ref[s, HALO:HALO + tt, :] = cur[:, sl]
        buf_ref[s, HALO + tt:, :] = nxt[:, sl]
    off = HALO - CONV_PAD
    for s in range(N_SLABS):
        sl = slice(s * LANES, (s + 1) * LANES)
        for r0 in range(0, tt, rb):
            acc = jnp.broadcast_to(bdw_ref[:, sl], (rb, LANES))
            for k in range(CONV_WIDTH):
                acc = acc + wdw_ref[k:k + 1, sl] * buf_ref[s, r0 + off + k:r0 + off + k + rb, :]
            cv_ref[r0:r0 + rb, sl] = acc
    h = cv_ref[...]
    mu = jnp.mean(h, axis=-1, keepdims=True)
    hc = h - mu
    var = jnp.mean(hc * hc, axis=-1, keepdims=True)
    y = hc * lax.rsqrt(var + EPS) * lg_ref[...] + lb_ref[...]
    y = jax.nn.silu(y).astype(BF16)
    o_ref[0] = x_ref[0] + jnp.dot(y, w_ref[...], preferred_element_type=F32) + b_ref[...]


def _conv_out(u, x, w_dw, b_dw, ln_g, ln_b, w_out, b_out):
    bsz, seq, _ = x.shape
    tt = min(256, seq)
    assert seq % tt == 0 and tt % HALO == 0
    rb = min(64, tt)
    n_halo = seq // HALO
    per = tt // HALO
    row = lambda v: v.reshape(1, D_MODEL)
    return pl.pallas_call(
        functools.partial(_conv_out_kernel, tt=tt, rb=rb),
        grid=(bsz, seq // tt),
        in_specs=[
            pl.BlockSpec((1, tt, D_MODEL), lambda b, t: (b, t, 0)),
            pl.BlockSpec((1, HALO, D_MODEL), lambda b, t: (b, jnp.maximum(t * per - 1, 0), 0)),
            pl.BlockSpec((1, HALO, D_MODEL), lambda b, t: (b, jnp.minimum((t + 1) * per, n_halo - 1), 0)),
            pl.BlockSpec((1, tt, D_MODEL), lambda b, t: (b, t, 0)),
            _const_spec((CONV_WIDTH, D_MODEL)),
            _const_spec((1, D_MODEL)),
            _const_spec((1, D_MODEL)),
            _const_spec((1, D_MODEL)),
            _const_spec((D_MODEL, D_MODEL)),
            _const_spec((1, D_MODEL)),
        ],
        out_specs=pl.BlockSpec((1, tt, D_MODEL), lambda b, t: (b, t, 0)),
        out_shape=jax.ShapeDtypeStruct(x.shape, F32),
        scratch_shapes=[pltpu.VMEM((N_SLABS, tt + 2 * HALO, LANES), F32), pltpu.VMEM((tt, D_MODEL), F32)],
        compiler_params=_cparams(2),
        name="conv_out",
    )(u, u, u, x, w_dw, row(b_dw), row(ln_g), row(ln_b), w_out, row(b_out))


def _attn_kernel(x_ref, k_ref, v_ref, g_ref, wq_ref, wo_ref, o_ref):
    x = x_ref[0]
    xn = _rms(x, g_ref[...]).astype(BF16)
    q = jnp.dot(xn, wq_ref[...], preferred_element_type=F32)
    q = (q * (XHEAD_DIM ** -0.5)).astype(BF16)
    heads = []
    for h in range(N_XHEADS):
        sl = slice(h * XHEAD_DIM, (h + 1) * XHEAD_DIM)
        s = lax.dot_general(q[:, sl], k_ref[0, 0, :, sl], (((1,), (1,)), ((), ())),
                            preferred_element_type=F32)
        e = jnp.exp(s - jnp.max(s, axis=-1, keepdims=True))
        p = (e / jnp.sum(e, axis=-1, keepdims=True)).astype(BF16)
        heads.append(jnp.dot(p, v_ref[0, 0, :, sl], preferred_element_type=F32).astype(BF16))
    o = jnp.concatenate(heads, axis=-1)
    o_ref[0] = x + jnp.dot(o, wo_ref[...], preferred_element_type=F32)


def _cross_attention(x, k_all, v_all, layer, g, wq, wo):
    bsz, seq, _ = x.shape
    n_mem = k_all.shape[2]
    tt = _token_tile(seq)
    kv_spec = pl.BlockSpec((1, 1, n_mem, D_MODEL), lambda b, t: (layer, b, 0, 0))
    return pl.pallas_call(
        _attn_kernel,
        grid=(bsz, seq // tt),
        in_specs=[
            pl.BlockSpec((1, tt, D_MODEL), lambda b, t: (b, t, 0)),
            kv_spec, kv_spec,
            _const_spec((1, D_MODEL)),
            _const_spec((D_MODEL, D_MODEL)),
            _const_spec((D_MODEL, D_MODEL)),
        ],
        out_specs=pl.BlockSpec((1, tt, D_MODEL), lambda b, t: (b, t, 0)),
        out_shape=jax.ShapeDtypeStruct(x.shape, F32),
        compiler_params=_cparams(2),
        name="cross_attn",
    )(x, k_all, v_all, g.reshape(1, D_MODEL), wq, wo)


def _mlp_kernel(x_ref, g_ref, win_ref, wout_ref, gn_ref, o_ref, *rest, mode):
    x = x_ref[...]
    xn = _rms(x, g_ref[...]).astype(BF16)
    acc = x
    for c in range(D_FF // D_MODEL):
        sl = slice(c * D_MODEL, (c + 1) * D_MODEL)
        h = jnp.dot(xn, win_ref[:, sl], preferred_element_type=F32)
        h = jnp.square(jnp.maximum(h, 0.0)).astype(BF16)
        acc = acc + jnp.dot(h, wout_ref[sl, :], preferred_element_type=F32)
    if mode == "final":
        o_ref[...] = _rms(acc, gn_ref[...])
        return
    o_ref[...] = acc
    if mode == "s5_next":
        _store_rms_slabs(acc, gn_ref[...], rest[0])


def _mlp(x2, g, w_in, w_out, g_next, mode):
    n_tok = x2.shape[0]
    tt = _token_tile(n_tok)
    out_specs = [pl.BlockSpec((tt, D_MODEL), lambda t: (t, 0))]
    out_shape = [jax.ShapeDtypeStruct((n_tok, D_MODEL), F32)]
    if mode == "s5_next":
        out_specs.append(pl.BlockSpec((N_SLABS, tt, LANES), lambda t: (0, t, 0)))
        out_shape.append(jax.ShapeDtypeStruct((N_SLABS, n_tok, LANES), BF16))
    return pl.pallas_call(
        functools.partial(_mlp_kernel, mode=mode),
        grid=(n_tok // tt,),
        in_specs=[
            pl.BlockSpec((tt, D_MODEL), lambda t: (t, 0)),
            _const_spec((1, D_MODEL)),
            _const_spec((D_MODEL, D_FF)),
            _const_spec((D_FF, D_MODEL)),
            _const_spec((1, D_MODEL)),
        ],
        out_specs=out_specs,
        out_shape=out_shape,
        compiler_params=_cparams(1),
        name="mlp_" + mode,
    )(x2, g.reshape(1, D_MODEL), w_in, w_out, g_next.reshape(1, D_MODEL))


def _trunk(x, mem, p, s5_preps):
    bsz, seq, _ = x.shape
    depth = p['attn_w_q'].shape[0]
    k_all, v_all = _kv_all_layers(mem, p['norm_mem'], p['attn_w_kv'])
    flat = lambda a: a.reshape(bsz * seq, D_MODEL)
    u = None
    for i in range(depth):
        j = i // 2
        if i % 2 == 0:
            if u is None:
                u = _rms_to_slabs(flat(x), p['norm_mix'][i])
            g = _s5_scan(u, seq, s5_preps[j])
            x = _glu_residual(g, flat(x), p['ssm_w_glu'][j], p['ssm_b_glu'][j]).reshape(x.shape)
        else:
            u = _conv_in(flat(x), p['norm_mix'][i], p['conv_w_in'][j], p['conv_b_in'][j])
            x = _conv_out(u.reshape(x.shape), x, p['conv_w_dw'][j], p['conv_b_dw'][j],
                          p['conv_ln_g'][j], p['conv_ln_b'][j], p['conv_w_out'][j], p['conv_b_out'][j])
        x = _cross_attention(x, k_all, v_all, i, p['norm_xq'][i], p['attn_w_q'][i], p['attn_w_o'][i])
        if i == depth - 1:
            mode, g_next = "final", p['norm_final']
        elif (i + 1) % 2 == 0:
            mode, g_next = "s5_next", p['norm_mix'][i + 1]
        else:
            mode, g_next = "plain", p['norm_final']
        out = _mlp(flat(x), p['norm_ffn'][i], p['mlp_w_in'][i], p['mlp_w_out'][i], g_next, mode)
        x = out[0].reshape(x.shape)
        u = out[1] if mode == "s5_next" else None
    return x


def kernel(x_prompt, x_sample, mem_prompt, mem_sample, norm_mix, norm_xq, norm_mem, norm_ffn, norm_final, ssm_a_re, ssm_a_im, ssm_log_step, ssm_b_re, ssm_b_im, ssm_c_re, ssm_c_im, ssm_d, ssm_w_glu, ssm_b_glu, conv_w_in, conv_b_in, conv_w_dw, conv_b_dw, conv_ln_g, conv_ln_b, conv_w_out, conv_b_out, attn_w_q, attn_w_kv, attn_w_o, mlp_w_in, mlp_w_out):
    p = dict(norm_mix=norm_mix, norm_xq=norm_xq, norm_mem=norm_mem, norm_ffn=norm_ffn,
             norm_final=norm_final, ssm_b_glu=ssm_b_glu,
             conv_b_in=conv_b_in, conv_w_dw=conv_w_dw, conv_b_dw=conv_b_dw,
             conv_ln_g=conv_ln_g, conv_ln_b=conv_ln_b, conv_b_out=conv_b_out)
    for name, w in (('ssm_w_glu', ssm_w_glu), ('conv_w_in', conv_w_in), ('conv_w_out', conv_w_out),
                    ('attn_w_q', attn_w_q), ('attn_w_kv', attn_w_kv), ('attn_w_o', attn_w_o),
                    ('mlp_w_in', mlp_w_in), ('mlp_w_out', mlp_w_out)):
        p[name] = w.astype(BF16)
    s5_preps = [_s5_prepare(ssm_a_re[j], ssm_a_im[j], ssm_log_step[j], ssm_b_re[j], ssm_b_im[j],
                            ssm_c_re[j], ssm_c_im[j], ssm_d[j]) for j in range(ssm_a_re.shape[0])]
    y_prompt = _trunk(x_prompt, mem_prompt, p, s5_preps)
    y_sample = _trunk(x_sample, mem_sample, p, s5_preps)
    return (y_prompt, y_sample)
```

```python
import functools
import math

import jax
import jax.numpy as jnp
from jax import lax
from jax.experimental import pallas as pl
from jax.experimental.pallas import tpu as pltpu

D_MODEL = 1024
GROUP_CH = 16
N_GROUPS = D_MODEL // GROUP_CH
STATE = 64
CONV_WIDTH = 31
CONV_PAD = (CONV_WIDTH - 1) // 2
N_XHEADS = 4
XHEAD_DIM = D_MODEL // N_XHEADS
D_FF = 4 * D_MODEL
EPS = 1e-6

LANES = 128
N_SLABS = D_MODEL // LANES
GROUPS_PER_SLAB = LANES // GROUP_CH
S5_CHUNK = 16
N_HALF = S5_CHUNK // GROUPS_PER_SLAB
PAIR = 2
N_PAIRS = GROUPS_PER_SLAB // PAIR
PAIR_COLS = PAIR * S5_CHUNK * GROUP_CH
PAIR_STATE = 4 * PAIR * STATE
N_STATE_BLOCKS = 2 * N_PAIRS
STAGE_SLOTS = 4
HALO = 16
VMEM_LIMIT = 56 * 1024 * 1024

F32 = jnp.float32
BF16 = jnp.bfloat16


def _cparams(n_axes):
    return pltpu.CompilerParams(
        dimension_semantics=("arbitrary",) * n_axes, vmem_limit_bytes=VMEM_LIMIT)


def _const_spec(shape):
    nd = len(shape)
    return pl.BlockSpec(shape, lambda *_: (0,) * nd, pipeline_mode=pl.Buffered(1))


def _rms(x, g):
    return x * lax.rsqrt(jnp.mean(x * x, axis=-1, keepdims=True) + EPS) * g


def _token_tile(n):
    t = min(512, n)
    assert n % t == 0
    return t


def _row_parts(n):
    parts = 2 if n % 32 == 0 else 1
    return [slice(i * n // parts, (i + 1) * n // parts) for i in range(parts)]


def _kv_kernel(mem_ref, g_ref, w_ref, k_ref, v_ref):
    mn = _rms(mem_ref[0], g_ref[0]).astype(BF16)
    kv = jnp.dot(mn, w_ref[0], preferred_element_type=F32)
    k_ref[0, 0] = kv[:, :D_MODEL].astype(BF16)
    v_ref[0, 0] = kv[:, D_MODEL:].astype(BF16)


def _kv_all_layers(mem, norm_mem, w_kv):
    bsz, n_mem, _ = mem.shape
    depth = w_kv.shape[0]
    out = jax.ShapeDtypeStruct((depth, bsz, n_mem, D_MODEL), BF16)
    return pl.pallas_call(
        _kv_kernel,
        grid=(depth, bsz),
        in_specs=[
            pl.BlockSpec((1, n_mem, D_MODEL), lambda i, b: (b, 0, 0)),
            pl.BlockSpec((1, 1, D_MODEL), lambda i, b: (i, 0, 0)),
            pl.BlockSpec((1, D_MODEL, 2 * D_MODEL), lambda i, b: (i, 0, 0)),
        ],
        out_specs=[
            pl.BlockSpec((1, 1, n_mem, D_MODEL), lambda i, b: (i, b, 0, 0)),
            pl.BlockSpec((1, 1, n_mem, D_MODEL), lambda i, b: (i, b, 0, 0)),
        ],
        out_shape=[out, out],
        compiler_params=_cparams(2),
        name="kv_proj",
    )(mem, norm_mem.reshape(depth, 1, D_MODEL), w_kv)


def _store_rms_slabs(x, g, o_ref):
    xn = _rms(x, g).astype(BF16)
    for s in range(N_SLABS):
        o_ref[s] = xn[:, s * LANES:(s + 1) * LANES]


def _rms_slab_kernel(x_ref, g_ref, o_ref):
    _store_rms_slabs(x_ref[...], g_ref[...], o_ref)


def _rms_to_slabs(x2, g):
    n_tok = x2.shape[0]
    tt = _token_tile(n_tok)
    return pl.pallas_call(
        _rms_slab_kernel,
        grid=(n_tok // tt,),
        in_specs=[pl.BlockSpec((tt, D_MODEL), lambda t: (t, 0)), _const_spec((1, D_MODEL))],
        out_specs=pl.BlockSpec((N_SLABS, tt, LANES), lambda t: (0, t, 0)),
        out_shape=jax.ShapeDtypeStruct((N_SLABS, n_tok, LANES), BF16),
        compiler_params=_cparams(1),
        name="rms_slabs",
    )(x2, g.reshape(1, D_MODEL))


def _lane_block_transpose(vs):
    vs = list(vs)
    blk = lax.broadcasted_iota(jnp.int32, vs[0].shape, 1) // GROUP_CH
    d = 1
    while d < len(vs):
        upper = (blk & d) != 0
        for i in range(len(vs)):
            if i & d:
                continue
            a, b = vs[i], vs[i | d]
            vs[i] = jnp.where(upper, pltpu.roll(b, d * GROUP_CH, 1), a)
            vs[i | d] = jnp.where(upper, b, pltpu.roll(a, LANES - d * GROUP_CH, 1))
        d *= 2
    return vs


def _s5_kernel(u_ref, win_ref, wi_ref, wo_ref, lam_ref, d_ref, o_ref,
               ulhs_ref, g_ref, stage_ref, zs_ref, sf_ref, sb_ref, *, bb, nc, rt, rs, zp, pitch):
    tiles = [(b * nc + k0, b * pitch + k0) for b in range(bb) for k0 in range(0, nc, rt)]
    as_words = lambda v: pltpu.bitcast(v, jnp.int32)
    as_bf16 = lambda v: pltpu.bitcast(v, BF16)

    n_sub = 0
    for r0, p0 in tiles:
        for q0 in range(0, rt, rs):
            tok0 = (r0 + q0) * S5_CHUNK
            stage = stage_ref.at[n_sub % stage_ref.shape[0]]
            n_sub += 1
            stage[...] = u_ref[0, tok0:tok0 + rs * S5_CHUNK, :].astype(F32)
            steps = [as_words(stage[pl.ds(t, rs, stride=S5_CHUNK), :].astype(BF16))
                     for t in range(S5_CHUNK)]
            runs = [_lane_block_transpose(steps[h * GROUPS_PER_SLAB:(h + 1) * GROUPS_PER_SLAB])
                    for h in range(N_HALF)]
            for pp in range(N_PAIRS):
                ulhs_ref[pp, r0 + q0:r0 + q0 + rs, :] = jnp.concatenate(
                    [as_bf16(runs[h][PAIR * pp + g]) for g in range(PAIR) for h in range(N_HALF)], axis=-1)
        for pp in range(N_PAIRS):
            s = jnp.dot(ulhs_ref[pp, r0:r0 + rt, :], win_ref[0, pp], preferred_element_type=F32)
            for c in range(2):
                sf_ref[2 * pp + c, p0:p0 + rt, :] = s[:, c * LANES:(c + 1) * LANES]
                sb_ref[2 * pp + c, p0:p0 + rt, :] = s[:, (2 + c) * LANES:(3 + c) * LANES]

    lam = lam_ref[0]

    def rows_of_chunk(k):
        return pl.ds(k, bb, stride=pitch) if bb > 1 else pl.ds(k, 1)

    def advance(s_ref, lam_row, idx, carry):
        new = [None] * N_STATE_BLOCKS
        for pp in range(N_PAIRS):
            re, im = 2 * pp, 2 * pp + 1
            l_re = lam_row[:, re * LANES:(re + 1) * LANES]
            l_im = lam_row[:, im * LANES:(im + 1) * LANES]
            s_re, s_im = s_ref[re, idx, :], s_ref[im, idx, :]
            c_re, c_im = carry[re], carry[im]
            s_ref[re, idx, :] = c_re
            s_ref[im, idx, :] = c_im
            new[re] = l_re * c_re - l_im * c_im + s_re
            new[im] = l_re * c_im + l_im * c_re + s_im
        return tuple(new)

    def body(k, carry):
        cf, cb = carry
        return (advance(sf_ref, lam[0:1], rows_of_chunk(k), cf),
                advance(sb_ref, lam[1:2], rows_of_chunk(nc - 1 - k), cb))

    zero = tuple(jnp.zeros((bb, LANES), F32) for _ in range(N_STATE_BLOCKS))
    lax.fori_loop(0, nc, body, (zero, zero))

    n_sub = 0
    for n_tile, (r0, p0) in enumerate(tiles):
        g_tile = g_ref.at[n_tile % g_ref.shape[0]]
        for pp in range(N_PAIRS):
            u = ulhs_ref[pp, r0:r0 + rt, :]
            entering = jnp.concatenate([s_ref[2 * pp + c, p0:p0 + rt, :]
                                        for s_ref in (sf_ref, sb_ref) for c in range(2)], axis=-1).astype(BF16)
            y = jnp.dot(u, wi_ref[0, pp], preferred_element_type=F32)
            y = y + jnp.dot(entering, wo_ref[0, pp], preferred_element_type=F32)
            g_tile[pp, 0:rt, :] = jax.nn.gelu(y + d_ref[0, pp] * u.astype(F32)).astype(BF16)
        for q0 in range(0, rt, rs):
            zs = zs_ref.at[n_sub % zs_ref.shape[0]]
            n_sub += 1
            for h in range(N_HALF):
                run = _lane_block_transpose([
                    as_words(g_tile[q // PAIR, q0:q0 + rs,
                                    ((q % PAIR) * N_HALF + h) * LANES:((q % PAIR) * N_HALF + h + 1) * LANES])
                    for q in range(GROUPS_PER_SLAB)])
                for i in range(GROUPS_PER_SLAB):
                    z0 = (h * GROUPS_PER_SLAB + i) * zp
                    zs[z0:z0 + rs, :] = as_bf16(run[i]).astype(F32)
            tok0 = (r0 + q0) * S5_CHUNK
            o_ref[0, tok0:tok0 + rs * S5_CHUNK, :] = jnp.concatenate(
                [zs[pl.ds(k, S5_CHUNK, stride=zp), :] for k in range(rs)], axis=0).astype(BF16)


def _s5_prepare(a_re, a_im, log_step, b_re, b_im, c_re, c_im, d_skip):
    t_len = S5_CHUNK
    step = jnp.exp(log_step.astype(F32))[..., None]
    ar, ai = a_re.astype(F32) * step, a_im.astype(F32) * step
    n = jnp.arange(t_len + 1, dtype=F32)[None, :, None, None]
    mag = jnp.exp(ar[:, None] * n)
    pw_re, pw_im = mag * jnp.cos(ai[:, None] * n), mag * jnp.sin(ai[:, None] * n)
    num_re, num_im = pw_re[:, 1] - 1.0, pw_im[:, 1]
    den = a_re * a_re + a_im * a_im
    z_re = (num_re * a_re + num_im * a_im) / den
    z_im = (num_im * a_re - num_re * a_im) / den
    bb_re = z_re[..., None] * b_re - z_im[..., None] * b_im
    bb_im = z_re[..., None] * b_im + z_im[..., None] * b_re

    t_idx = jnp.arange(t_len)
    n_pairs = N_GROUPS // PAIR
    tc = t_len * GROUP_CH
    cat = lambda *xs: jnp.concatenate(xs, axis=-1)

    def powers(d, idx):
        return tuple(v[d][idx].reshape(t_len, n_pairs, LANES).transpose(1, 0, 2) for v in (pw_re, pw_im))

    def b_rows(x):
        return x.reshape(n_pairs, PAIR, STATE, GROUP_CH).transpose(0, 3, 1, 2).reshape(n_pairs, GROUP_CH, LANES)

    def c_rows(x):
        return x.reshape(n_pairs, PAIR, GROUP_CH, STATE).transpose(0, 2, 1, 3).reshape(n_pairs, GROUP_CH, LANES)

    def pair_operator(p1, r1, p2, r2):
        x = p1[:, :, None, :] * r1[:, None, :, :] + p2[:, :, None, :] * r2[:, None, :, :]
        shape = (1, PAIR, 1, 1, PAIR_STATE)
        same = lax.broadcasted_iota(jnp.int32, shape, 1) == (lax.broadcasted_iota(jnp.int32, shape, 4) // STATE) % PAIR
        return jnp.where(same, x[:, None], 0.0).reshape(N_SLABS, N_PAIRS, PAIR_COLS, PAIR_STATE).astype(BF16)

    (pr_f, pi_f), (pr_b, pi_b) = powers(0, t_len - 1 - t_idx), powers(1, t_idx)
    br_f, bi_f, br_b, bi_b = b_rows(bb_re[0]), b_rows(bb_im[0]), b_rows(bb_re[1]), b_rows(bb_im[1])
    w_in = pair_operator(cat(pr_f, pr_f, pr_b, pr_b), cat(br_f, bi_f, br_b, bi_b),
                         cat(-pi_f, pi_f, -pi_b, pi_b), cat(bi_f, br_f, bi_b, br_b))

    (qr_f, qi_f), (qr_b, qi_b) = powers(0, t_idx + 1), powers(1, t_len - t_idx)
    cr_f, ci_f, cr_b, ci_b = c_rows(c_re[0]), c_rows(c_im[0]), c_rows(c_re[1]), c_rows(c_im[1])
    w_out_t = pair_operator(cat(qr_f, -qi_f, qr_b, -qi_b), cat(cr_f, cr_f, cr_b, cr_b),
                            cat(-qi_f, -qr_f, -qi_b, -qr_b), cat(ci_f, ci_f, ci_b, ci_b))

    def impulse(d, lags):
        pr, pi = (v[d][lags].transpose(1, 0, 2) for v in (pw_re, pw_im))
        z = (cat(pr, -pi)[:, :, None, :] * cat(c_re[d], c_re[d])[:, None, :, :]
             + cat(-pi, -pr)[:, :, None, :] * cat(c_im[d], c_im[d])[:, None, :, :])
        b = cat(bb_re[d].transpose(0, 2, 1), bb_im[d].transpose(0, 2, 1))
        return jnp.einsum('gcq,gmq->gcm', b, z.reshape(N_GROUPS, tc, 2 * STATE), precision=lax.Precision.HIGHEST)

    k_f, k_b = impulse(0, t_idx), impulse(1, t_len - 1 - t_idx)
    col = lax.broadcasted_iota(jnp.int32, (1, 1, tc), 2)
    toe = jnp.stack([
        jnp.where(col >= GROUP_CH * t, jnp.roll(k_f, GROUP_CH * t, axis=-1), 0.0)
        + jnp.where(col < GROUP_CH * (t + 1), jnp.roll(k_b, -GROUP_CH * (t_len - 1 - t), axis=-1), 0.0)
        for t in range(t_len)], axis=1)
    eye = jnp.eye(PAIR, dtype=F32)
    w_intra = (toe.reshape(n_pairs, PAIR, tc, 1, tc) * eye[None, :, None, :, None]).reshape(
        N_SLABS, N_PAIRS, PAIR_COLS, PAIR_COLS).astype(BF16)

    def pair_state(x_re, x_im):
        blocks = [v.reshape(N_SLABS, N_PAIRS, 1, PAIR * STATE) for v in (x_re, x_im)]
        return jnp.concatenate(blocks, axis=2).reshape(N_SLABS, N_STATE_BLOCKS * LANES)

    lam_t = jnp.stack([pair_state(pw_re[0, t_len], pw_im[0, t_len]),
                       pair_state(pw_re[1, t_len], pw_im[1, t_len])], axis=1)
    d = jnp.broadcast_to(d_skip.astype(F32).reshape(N_SLABS, N_PAIRS, PAIR, 1, GROUP_CH),
                         (N_SLABS, N_PAIRS, PAIR, t_len, GROUP_CH)).reshape(N_SLABS, N_PAIRS, 1, PAIR_COLS)
    return w_in, w_intra, jnp.swapaxes(w_out_t, 2, 3), lam_t, d


def _s5_scan(u_slabs, seq, prep):
    w_in, w_intra, w_out, lam_t, d = prep
    n_tok = u_slabs.shape[1]
    nc = seq // S5_CHUNK
    n_seq = n_tok // seq
    bb = max(1, min(n_seq, 1024 // nc))
    assert n_seq % bb == 0 and seq % S5_CHUNK == 0
    rows = bb * nc
    rt = min(128, nc)
    rs = min(64, rt)
    assert nc % rt == 0 and rt % rs == 0
    zp = rs + 8
    pitch = nc + 4 if nc % 8 == 0 else nc
    pair_w = pl.BlockSpec((1, N_PAIRS, PAIR_COLS, PAIR_COLS), lambda s, r: (s, 0, 0, 0))
    tok_spec = pl.BlockSpec((1, rows * S5_CHUNK, LANES), lambda s, r: (s, r, 0))
    state = pltpu.VMEM((N_STATE_BLOCKS, bb * pitch, LANES), F32)
    return pl.pallas_call(
        functools.partial(_s5_kernel, bb=bb, nc=nc, rt=rt, rs=rs, zp=zp, pitch=pitch),
        grid=(N_SLABS, n_seq // bb),
        in_specs=[
            tok_spec, pair_w, pair_w, pair_w,
            pl.BlockSpec((1, 2, N_STATE_BLOCKS * LANES), lambda s, r: (s, 0, 0)),
            pl.BlockSpec((1, N_PAIRS, 1, PAIR_COLS), lambda s, r: (s, 0, 0, 0)),
        ],
        out_specs=tok_spec,
        out_shape=jax.ShapeDtypeStruct(u_slabs.shape, BF16),
        scratch_shapes=[pltpu.VMEM((N_PAIRS, rows, PAIR_COLS), BF16),
                        pltpu.VMEM((2, N_PAIRS, rt, PAIR_COLS), BF16),
                        pltpu.VMEM((STAGE_SLOTS, rs * S5_CHUNK, LANES), F32),
                        pltpu.VMEM((STAGE_SLOTS, S5_CHUNK * zp, LANES), F32), state, state],
        compiler_params=_cparams(2),
        name="s5_scan",
    )(u_slabs, w_in, w_intra, w_out, lam_t, d)


def _glu_kernel(g_ref, x_ref, w_ref, b_ref, o_ref):
    for rows in _row_parts(x_ref.shape[0]):
        g = jnp.concatenate([g_ref[s, rows, :] for s in range(N_SLABS)], axis=-1)
        h = jnp.dot(g, w_ref[...], preferred_element_type=F32) + b_ref[...]
        o_ref[rows, :] = x_ref[rows, :] + h[:, :D_MODEL] * jax.nn.sigmoid(h[:, D_MODEL:])


def _glu_residual(g_slabs, x2, w, b):
    n_tok = x2.shape[0]
    tt = _token_tile(n_tok)
    return pl.pallas_call(
        _glu_kernel,
        grid=(n_tok // tt,),
        in_specs=[
            pl.BlockSpec((N_SLABS, tt, LANES), lambda t: (0, t, 0)),
            pl.BlockSpec((tt, D_MODEL), lambda t: (t, 0)),
            _const_spec((D_MODEL, 2 * D_MODEL)),
            _const_spec((1, 2 * D_MODEL)),
        ],
        out_specs=pl.BlockSpec((tt, D_MODEL), lambda t: (t, 0)),
        out_shape=jax.ShapeDtypeStruct((n_tok, D_MODEL), F32),
        compiler_params=_cparams(1),
        name="s5_glu",
    )(g_slabs, x2, w, b.reshape(1, 2 * D_MODEL))


def _conv_in_kernel(x_ref, g_ref, w_ref, b_ref, o_ref):
    for rows in _row_parts(x_ref.shape[0]):
        xn = _rms(x_ref[rows, :], g_ref[...]).astype(BF16)
        h = jnp.dot(xn, w_ref[...], preferred_element_type=F32) + b_ref[...]
        o_ref[rows, :] = (h[:, :D_MODEL] * jax.nn.sigmoid(h[:, D_MODEL:])).astype(BF16)


def _conv_in(x2, g, w, b):
    n_tok = x2.shape[0]
    tt = _token_tile(n_tok)
    return pl.pallas_call(
        _conv_in_kernel,
        grid=(n_tok // tt,),
        in_specs=[
            pl.BlockSpec((tt, D_MODEL), lambda t: (t, 0)),
            _const_spec((1, D_MODEL)),
            _const_spec((D_MODEL, 2 * D_MODEL)),
            _const_spec((1, 2 * D_MODEL)),
        ],
        out_specs=pl.BlockSpec((tt, D_MODEL), lambda t: (t, 0)),
        out_shape=jax.ShapeDtypeStruct((n_tok, D_MODEL), BF16),
        compiler_params=_cparams(1),
        name="conv_in",
    )(x2, g.reshape(1, D_MODEL), w, b.reshape(1, 2 * D_MODEL))


def _conv_out_kernel(u_ref, up_ref, un_ref, x_ref, wdw_ref, bdw_ref, lg_ref, lb_ref, w_ref, b_ref,
                     o_ref, buf_ref, cv_ref, *, tt, rb):
    t = pl.program_id(1)
    last = pl.num_programs(1) - 1
    prev = jnp.where(t > 0, up_ref[0].astype(F32), 0.0)
    cur = u_ref[0].astype(F32)
    nxt = jnp.where(t < last, un_ref[0].astype(F32), 0.0)
    for s in range(N_SLABS):
        sl = slice(s * LANES, (s + 1) * LANES)
        buf_ref[s, 0:HALO, :] = prev[:, sl]
        buf_ref[s, HALO:HALO + tt, :] = cur[:, sl]
        buf_ref[s, HALO + tt:, :] = nxt[:, sl]
    off = HALO - CONV_PAD
    for s in range(N_SLABS):
        sl = slice(s * LANES, (s + 1) * LANES)
        for r0 in range(0, tt, rb):
            acc = jnp.broadcast_to(bdw_ref[:, sl], (rb, LANES))
            for k in range(CONV_WIDTH):
                acc = acc + wdw_ref[k:k + 1, sl] * buf_ref[s, r0 + off + k:r0 + off + k + rb, :]
            cv_ref[r0:r0 + rb, sl] = acc
    h = cv_ref[...]
    mu = jnp.mean(h, axis=-1, keepdims=True)
    hc = h - mu
    var = jnp.mean(hc * hc, axis=-1, keepdims=True)
    y = hc * lax.rsqrt(var + EPS) * lg_ref[...] + lb_ref[...]
    y = jax.nn.silu(y).astype(BF16)
    o_ref[0] = x_ref[0] + jnp.dot(y, w_ref[...], preferred_element_type=F32) + b_ref[...]


def _conv_out(u, x, w_dw, b_dw, ln_g, ln_b, w_out, b_out):
    bsz, seq, _ = x.shape
    tt = _token_tile(seq)
    assert tt % HALO == 0
    rb = min(64, tt)
    n_halo = seq // HALO
    per = tt // HALO
    row = lambda v: v.reshape(1, D_MODEL)
    return pl.pallas_call(
        functools.partial(_conv_out_kernel, tt=tt, rb=rb),
        grid=(bsz, seq // tt),
        in_specs=[
            pl.BlockSpec((1, tt, D_MODEL), lambda b, t: (b, t, 0)),
            pl.BlockSpec((1, HALO, D_MODEL), lambda b, t: (b, jnp.maximum(t * per - 1, 0), 0)),
            pl.BlockSpec((1, HALO, D_MODEL), lambda b, t: (b, jnp.minimum((t + 1) * per, n_halo - 1), 0)),
            pl.BlockSpec((1, tt, D_MODEL), lambda b, t: (b, t, 0)),
            _const_spec((CONV_WIDTH, D_MODEL)),
            _const_spec((1, D_MODEL)),
            _const_spec((1, D_MODEL)),
            _const_spec((1, D_MODEL)),
            _const_spec((D_MODEL, D_MODEL)),
            _const_spec((1, D_MODEL)),
        ],
        out_specs=pl.BlockSpec((1, tt, D_MODEL), lambda b, t: (b, t, 0)),
        out_shape=jax.ShapeDtypeStruct(x.shape, F32),
        scratch_shapes=[pltpu.VMEM((N_SLABS, tt + 2 * HALO, LANES), F32), pltpu.VMEM((tt, D_MODEL), F32)],
        compiler_params=_cparams(2),
        name="conv_out",
    )(u, u, u, x, w_dw, row(b_dw), row(ln_g), row(ln_b), w_out, row(b_out))


def _attn_kernel(x_ref, k_ref, v_ref, g_ref, wq_ref, wo_ref, o_ref):
    x = x_ref[0]
    xn = _rms(x, g_ref[...]).astype(BF16)
    q = jnp.dot(xn, wq_ref[...], preferred_element_type=F32)
    q = (q * (XHEAD_DIM ** -0.5)).astype(BF16)
    heads = []
    for h in range(N_XHEADS):
        sl = slice(h * XHEAD_DIM, (h + 1) * XHEAD_DIM)
        s = lax.dot_general(q[:, sl], k_ref[0, 0, :, sl], (((1,), (1,)), ((), ())),
                            preferred_element_type=F32)
        e = jnp.exp(s - jnp.max(s, axis=-1, keepdims=True))
        p = (e / jnp.sum(e, axis=-1, keepdims=True)).astype(BF16)
        heads.append(jnp.dot(p, v_ref[0, 0, :, sl], preferred_element_type=F32).astype(BF16))
    o = jnp.concatenate(heads, axis=-1)
    o_ref[0] = x + jnp.dot(o, wo_ref[...], preferred_element_type=F32)


def _cross_attention(x, k_all, v_all, layer, g, wq, wo):
    bsz, seq, _ = x.shape
    n_mem = k_all.shape[2]
    tt = _token_tile(seq)
    kv_spec = pl.BlockSpec((1, 1, n_mem, D_MODEL), lambda b, t: (layer, b, 0, 0))
    return pl.pallas_call(
        _attn_kernel,
        grid=(bsz, seq // tt),
        in_specs=[
            pl.BlockSpec((1, tt, D_MODEL), lambda b, t: (b, t, 0)),
            kv_spec, kv_spec,
            _const_spec((1, D_MODEL)),
            _const_spec((D_MODEL, D_MODEL)),
            _const_spec((D_MODEL, D_MODEL)),
        ],
        out_specs=pl.BlockSpec((1, tt, D_MODEL), lambda b, t: (b, t, 0)),
        out_shape=jax.ShapeDtypeStruct(x.shape, F32),
        compiler_params=_cparams(2),
        name="cross_attn",
    )(x, k_all, v_all, g.reshape(1, D_MODEL), wq, wo)


def _mlp_kernel(x_ref, g_ref, win_ref, wout_ref, gn_ref, o_ref, *rest, mode):
    x = x_ref[...]
    xn = _rms(x, g_ref[...]).astype(BF16)
    acc = x
    for c in range(D_FF // D_MODEL):
        sl = slice(c * D_MODEL, (c + 1) * D_MODEL)
        h = jnp.dot(xn, win_ref[:, sl], preferred_element_type=F32)
        h = jnp.square(jnp.maximum(h, 0.0)).astype(BF16)
        acc = acc + jnp.dot(h, wout_ref[sl, :], preferred_element_type=F32)
    if mode == "final":
        o_ref[...] = _rms(acc, gn_ref[...])
        return
    o_ref[...] = acc
    if mode == "s5_next":
        _store_rms_slabs(acc, gn_ref[...], rest[0])


def _mlp(x2, g, w_in, w_out, g_next, mode):
    n_tok = x2.shape[0]
    tt = _token_tile(n_tok)
    out_specs = [pl.BlockSpec((tt, D_MODEL), lambda t: (t, 0))]
    out_shape = [jax.ShapeDtypeStruct((n_tok, D_MODEL), F32)]
    if mode == "s5_next":
        out_specs.append(pl.BlockSpec((N_SLABS, tt, LANES), lambda t: (0, t, 0)))
        out_shape.append(jax.ShapeDtypeStruct((N_SLABS, n_tok, LANES), BF16))
    return pl.pallas_call(
        functools.partial(_mlp_kernel, mode=mode),
        grid=(n_tok // tt,),
        in_specs=[
            pl.BlockSpec((tt, D_MODEL), lambda t: (t, 0)),
            _const_spec((1, D_MODEL)),
            _const_spec((D_MODEL, D_FF)),
            _const_spec((D_FF, D_MODEL)),
            _const_spec((1, D_MODEL)),
        ],
        out_specs=out_specs,
        out_shape=out_shape,
        compiler_params=_cparams(1),
        name="mlp_" + mode,
    )(x2, g.reshape(1, D_MODEL), w_in, w_out, g_next.reshape(1, D_MODEL))


def _trunk(x, mem, p, s5_preps):
    bsz, seq, _ = x.shape
    depth = p['attn_w_q'].shape[0]
    k_all, v_all = _kv_all_layers(mem, p['norm_mem'], p['attn_w_kv'])
    flat = lambda a: a.reshape(bsz * seq, D_MODEL)
    u = None
    for i in range(depth):
        j = i // 2
        if i % 2 == 0:
            if u is None:
                u = _rms_to_slabs(flat(x), p['norm_mix'][i])
            g = _s5_scan(u, seq, s5_preps[j])
            x = _glu_residual(g, flat(x), p['ssm_w_glu'][j], p['ssm_b_glu'][j]).reshape(x.shape)
        else:
            u = _conv_in(flat(x), p['norm_mix'][i], p['conv_w_in'][j], p['conv_b_in'][j])
            x = _conv_out(u.reshape(x.shape), x, p['conv_w_dw'][j], p['conv_b_dw'][j],
                          p['conv_ln_g'][j], p['conv_ln_b'][j], p['conv_w_out'][j], p['conv_b_out'][j])
        x = _cross_attention(x, k_all, v_all, i, p['norm_xq'][i], p['attn_w_q'][i], p['attn_w_o'][i])
        if i == depth - 1:
            mode, g_next = "final", p['norm_final']
        elif (i + 1) % 2 == 0:
            mode, g_next = "s5_next", p['norm_mix'][i + 1]
        else:
            mode, g_next = "plain", p['norm_final']
        out = _mlp(flat(x), p['norm_ffn'][i], p['mlp_w_in'][i], p['mlp_w_out'][i], g_next, mode)
        x = out[0].reshape(x.shape)
        u = out[1] if mode == "s5_next" else None
    return x


def kernel(x_prompt, x_sample, mem_prompt, mem_sample, norm_mix, norm_xq, norm_mem, norm_ffn, norm_final, ssm_a_re, ssm_a_im, ssm_log_step, ssm_b_re, ssm_b_im, ssm_c_re, ssm_c_im, ssm_d, ssm_w_glu, ssm_b_glu, conv_w_in, conv_b_in, conv_w_dw, conv_b_dw, conv_ln_g, conv_ln_b, conv_w_out, conv_b_out, attn_w_q, attn_w_kv, attn_w_o, mlp_w_in, mlp_w_out):
    p = dict(norm_mix=norm_mix, norm_xq=norm_xq, norm_mem=norm_mem, norm_ffn=norm_ffn,
             norm_final=norm_final, ssm_b_glu=ssm_b_glu,
             conv_b_in=conv_b_in, conv_w_dw=conv_w_dw, conv_b_dw=conv_b_dw,
             conv_ln_g=conv_ln_g, conv_ln_b=conv_ln_b, conv_b_out=conv_b_out)
    for name, w in (('ssm_w_glu', ssm_w_glu), ('conv_w_in', conv_w_in), ('conv_w_out', conv_w_out),
                    ('attn_w_q', attn_w_q), ('attn_w_kv', attn_w_kv), ('attn_w_o', attn_w_o),
                    ('mlp_w_in', mlp_w_in), ('mlp_w_out', mlp_w_out)):
        p[name] = w.astype(BF16)
    stacked = jax.vmap(_s5_prepare)(ssm_a_re, ssm_a_im, ssm_log_step, ssm_b_re, ssm_b_im,
                                    ssm_c_re, ssm_c_im, ssm_d)
    s5_preps = [tuple(a[j] for a in stacked) for j in range(ssm_a_re.shape[0])]
    y_prompt = _trunk(x_prompt, mem_prompt, p, s5_preps)
    y_sample = _trunk(x_sample, mem_sample, p, s5_preps)
    return (y_prompt, y_sample)
```

```python
import functools
import math

import jax
import jax.numpy as jnp
from jax import lax
from jax.experimental import pallas as pl
from jax.experimental.pallas import tpu as pltpu

D_MODEL = 1024
GROUP_CH = 16
N_GROUPS = D_MODEL // GROUP_CH
STATE = 64
CONV_WIDTH = 31
CONV_PAD = (CONV_WIDTH - 1) // 2
N_XHEADS = 4
XHEAD_DIM = D_MODEL // N_XHEADS
D_FF = 4 * D_MODEL
EPS = 1e-6

LANES = 128
N_SLABS = D_MODEL // LANES
GROUPS_PER_SLAB = LANES // GROUP_CH
S5_CHUNK = 16
N_HALF = S5_CHUNK // GROUPS_PER_SLAB
PAIR = 2
N_PAIRS = GROUPS_PER_SLAB // PAIR
PAIR_COLS = PAIR * S5_CHUNK * GROUP_CH
PAIR_STATE = 4 * PAIR * STATE
N_STATE_BLOCKS = 2 * N_PAIRS
STAGE_SLOTS = 4
HALO = 16
VMEM_LIMIT = 56 * 1024 * 1024

F32 = jnp.float32
BF16 = jnp.bfloat16


def _cparams(n_axes):
    return pltpu.CompilerParams(
        dimension_semantics=("arbitrary",) * n_axes, vmem_limit_bytes=VMEM_LIMIT)


def _const_spec(shape):
    nd = len(shape)
    return pl.BlockSpec(shape, lambda *_: (0,) * nd, pipeline_mode=pl.Buffered(1))


def _rms(x, g):
    return x * lax.rsqrt(jnp.mean(x * x, axis=-1, keepdims=True) + EPS) * g


def _token_tile(n, cap=512):
    t = min(cap, n)
    assert n % t == 0
    return t


BIG_TILE = 1024


def _row_parts(n):
    parts = 2 if n % 32 == 0 else 1
    return [slice(i * n // parts, (i + 1) * n // parts) for i in range(parts)]


def _kv_kernel(mem_ref, g_ref, w_ref, k_ref, v_ref):
    mn = _rms(mem_ref[0], g_ref[0]).astype(BF16)
    kv = jnp.dot(mn, w_ref[0], preferred_element_type=F32)
    k_ref[0, 0] = kv[:, :D_MODEL].astype(BF16)
    v_ref[0, 0] = kv[:, D_MODEL:].astype(BF16)


def _kv_all_layers(mem, norm_mem, w_kv):
    bsz, n_mem, _ = mem.shape
    depth = w_kv.shape[0]
    out = jax.ShapeDtypeStruct((depth, bsz, n_mem, D_MODEL), BF16)
    return pl.pallas_call(
        _kv_kernel,
        grid=(depth, bsz),
        in_specs=[
            pl.BlockSpec((1, n_mem, D_MODEL), lambda i, b: (b, 0, 0)),
            pl.BlockSpec((1, 1, D_MODEL), lambda i, b: (i, 0, 0)),
            pl.BlockSpec((1, D_MODEL, 2 * D_MODEL), lambda i, b: (i, 0, 0)),
        ],
        out_specs=[
            pl.BlockSpec((1, 1, n_mem, D_MODEL), lambda i, b: (i, b, 0, 0)),
            pl.BlockSpec((1, 1, n_mem, D_MODEL), lambda i, b: (i, b, 0, 0)),
        ],
        out_shape=[out, out],
        compiler_params=_cparams(2),
        name="kv_proj",
    )(mem, norm_mem.reshape(depth, 1, D_MODEL), w_kv)


def _store_rms_slabs(x, g, o_ref):
    xn = _rms(x, g).astype(BF16)
    for s in range(N_SLABS):
        o_ref[s] = xn[:, s * LANES:(s + 1) * LANES]


def _rms_slab_kernel(x_ref, g_ref, o_ref):
    _store_rms_slabs(x_ref[...], g_ref[...], o_ref)


def _rms_to_slabs(x2, g):
    n_tok = x2.shape[0]
    tt = _token_tile(n_tok)
    return pl.pallas_call(
        _rms_slab_kernel,
        grid=(n_tok // tt,),
        in_specs=[pl.BlockSpec((tt, D_MODEL), lambda t: (t, 0)), _const_spec((1, D_MODEL))],
        out_specs=pl.BlockSpec((N_SLABS, tt, LANES), lambda t: (0, t, 0)),
        out_shape=jax.ShapeDtypeStruct((N_SLABS, n_tok, LANES), BF16),
        compiler_params=_cparams(1),
        name="rms_slabs",
    )(x2, g.reshape(1, D_MODEL))


def _lane_block_transpose(vs):
    vs = list(vs)
    blk = lax.broadcasted_iota(jnp.int32, vs[0].shape, 1) // GROUP_CH
    d = 1
    while d < len(vs):
        upper = (blk & d) != 0
        for i in range(len(vs)):
            if i & d:
                continue
            a, b = vs[i], vs[i | d]
            vs[i] = jnp.where(upper, pltpu.roll(b, d * GROUP_CH, 1), a)
            vs[i | d] = jnp.where(upper, b, pltpu.roll(a, LANES - d * GROUP_CH, 1))
        d *= 2
    return vs


def _s5_kernel(u_ref, win_ref, wi_ref, wo_ref, lam_ref, d_ref, o_ref,
               ulhs_ref, g_ref, stage_ref, zs_ref, sf_ref, sb_ref, *, bb, nc, rt, rs, zp, pitch):
    tiles = [(b * nc + k0, b * pitch + k0) for b in range(bb) for k0 in range(0, nc, rt)]
    as_words = lambda v: pltpu.bitcast(v, jnp.int32)
    as_bf16 = lambda v: pltpu.bitcast(v, BF16)

    n_sub = 0
    for r0, p0 in tiles:
        for q0 in range(0, rt, rs):
            tok0 = (r0 + q0) * S5_CHUNK
            stage = stage_ref.at[n_sub % stage_ref.shape[0]]
            n_sub += 1
            stage[...] = u_ref[0, tok0:tok0 + rs * S5_CHUNK, :].astype(F32)
            steps = [as_words(stage[pl.ds(t, rs, stride=S5_CHUNK), :].astype(BF16))
                     for t in range(S5_CHUNK)]
            runs = [_lane_block_transpose(steps[h * GROUPS_PER_SLAB:(h + 1) * GROUPS_PER_SLAB])
                    for h in range(N_HALF)]
            for pp in range(N_PAIRS):
                ulhs_ref[pp, r0 + q0:r0 + q0 + rs, :] = jnp.concatenate(
                    [as_bf16(runs[h][PAIR * pp + g]) for g in range(PAIR) for h in range(N_HALF)], axis=-1)
        for pp in range(N_PAIRS):
            s = jnp.dot(ulhs_ref[pp, r0:r0 + rt, :], win_ref[0, pp], preferred_element_type=F32)
            for c in range(2):
                sf_ref[2 * pp + c, p0:p0 + rt, :] = s[:, c * LANES:(c + 1) * LANES]
                sb_ref[2 * pp + c, p0:p0 + rt, :] = s[:, (2 + c) * LANES:(3 + c) * LANES]

    lam = lam_ref[0]

    def rows_of_chunk(k):
        return pl.ds(k, bb, stride=pitch) if bb > 1 else pl.ds(k, 1)

    def advance(s_ref, lam_row, idx, carry):
        new = [None] * N_STATE_BLOCKS
        for pp in range(N_PAIRS):
            re, im = 2 * pp, 2 * pp + 1
            l_re = lam_row[:, re * LANES:(re + 1) * LANES]
            l_im = lam_row[:, im * LANES:(im + 1) * LANES]
            s_re, s_im = s_ref[re, idx, :], s_ref[im, idx, :]
            c_re, c_im = carry[re], carry[im]
            s_ref[re, idx, :] = c_re
            s_ref[im, idx, :] = c_im
            new[re] = l_re * c_re - l_im * c_im + s_re
            new[im] = l_re * c_im + l_im * c_re + s_im
        return tuple(new)

    def body(k, carry):
        cf, cb = carry
        return (advance(sf_ref, lam[0:1], rows_of_chunk(k), cf),
                advance(sb_ref, lam[1:2], rows_of_chunk(nc - 1 - k), cb))

    zero = tuple(jnp.zeros((bb, LANES), F32) for _ in range(N_STATE_BLOCKS))
    lax.fori_loop(0, nc, body, (zero, zero))

    n_sub = 0
    for n_tile, (r0, p0) in enumerate(tiles):
        g_tile = g_ref.at[n_tile % g_ref.shape[0]]
        for pp in range(N_PAIRS):
            u = ulhs_ref[pp, r0:r0 + rt, :]
            entering = jnp.concatenate([s_ref[2 * pp + c, p0:p0 + rt, :]
                                        for s_ref in (sf_ref, sb_ref) for c in range(2)], axis=-1).astype(BF16)
            y = jnp.dot(u, wi_ref[0, pp], preferred_element_type=F32)
            y = y + jnp.dot(entering, wo_ref[0, pp], preferred_element_type=F32)
            g_tile[pp, 0:rt, :] = jax.nn.gelu(y + d_ref[0, pp] * u.astype(F32)).astype(BF16)
        for q0 in range(0, rt, rs):
            zs = zs_ref.at[n_sub % zs_ref.shape[0]]
            n_sub += 1
            for h in range(N_HALF):
                run = _lane_block_transpose([
                    as_words(g_tile[q // PAIR, q0:q0 + rs,
                                    ((q % PAIR) * N_HALF + h) * LANES:((q % PAIR) * N_HALF + h + 1) * LANES])
                    for q in range(GROUPS_PER_SLAB)])
                for i in range(GROUPS_PER_SLAB):
                    z0 = (h * GROUPS_PER_SLAB + i) * zp
                    zs[z0:z0 + rs, :] = as_bf16(run[i]).astype(F32)
            tok0 = (r0 + q0) * S5_CHUNK
            o_ref[0, tok0:tok0 + rs * S5_CHUNK, :] = jnp.concatenate(
                [zs[pl.ds(k, S5_CHUNK, stride=zp), :] for k in range(rs)], axis=0).astype(BF16)


def _s5_prepare(a_re, a_im, log_step, b_re, b_im, c_re, c_im, d_skip):
    t_len = S5_CHUNK
    step = jnp.exp(log_step.astype(F32))[..., None]
    ar, ai = a_re.astype(F32) * step, a_im.astype(F32) * step
    n = jnp.arange(t_len + 1, dtype=F32)[None, :, None, None]
    mag = jnp.exp(ar[:, None] * n)
    pw_re, pw_im = mag * jnp.cos(ai[:, None] * n), mag * jnp.sin(ai[:, None] * n)
    num_re, num_im = pw_re[:, 1] - 1.0, pw_im[:, 1]
    den = a_re * a_re + a_im * a_im
    z_re = (num_re * a_re + num_im * a_im) / den
    z_im = (num_im * a_re - num_re * a_im) / den
    bb_re = z_re[..., None] * b_re - z_im[..., None] * b_im
    bb_im = z_re[..., None] * b_im + z_im[..., None] * b_re

    t_idx = jnp.arange(t_len)
    n_pairs = N_GROUPS // PAIR
    tc = t_len * GROUP_CH
    cat = lambda *xs: jnp.concatenate(xs, axis=-1)

    def powers(d, idx):
        return tuple(v[d][idx].reshape(t_len, n_pairs, LANES).transpose(1, 0, 2) for v in (pw_re, pw_im))

    def b_rows(x):
        return x.reshape(n_pairs, PAIR, STATE, GROUP_CH).transpose(0, 3, 1, 2).reshape(n_pairs, GROUP_CH, LANES)

    def c_rows(x):
        return x.reshape(n_pairs, PAIR, GROUP_CH, STATE).transpose(0, 2, 1, 3).reshape(n_pairs, GROUP_CH, LANES)

    def pair_operator(p1, r1, p2, r2):
        x = p1[:, :, None, :] * r1[:, None, :, :] + p2[:, :, None, :] * r2[:, None, :, :]
        shape = (1, PAIR, 1, 1, PAIR_STATE)
        same = lax.broadcasted_iota(jnp.int32, shape, 1) == (lax.broadcasted_iota(jnp.int32, shape, 4) // STATE) % PAIR
        return jnp.where(same, x[:, None], 0.0).reshape(N_SLABS, N_PAIRS, PAIR_COLS, PAIR_STATE).astype(BF16)

    (pr_f, pi_f), (pr_b, pi_b) = powers(0, t_len - 1 - t_idx), powers(1, t_idx)
    br_f, bi_f, br_b, bi_b = b_rows(bb_re[0]), b_rows(bb_im[0]), b_rows(bb_re[1]), b_rows(bb_im[1])
    w_in = pair_operator(cat(pr_f, pr_f, pr_b, pr_b), cat(br_f, bi_f, br_b, bi_b),
                         cat(-pi_f, pi_f, -pi_b, pi_b), cat(bi_f, br_f, bi_b, br_b))

    (qr_f, qi_f), (qr_b, qi_b) = powers(0, t_idx + 1), powers(1, t_len - t_idx)
    cr_f, ci_f, cr_b, ci_b = c_rows(c_re[0]), c_rows(c_im[0]), c_rows(c_re[1]), c_rows(c_im[1])
    w_out_t = pair_operator(cat(qr_f, -qi_f, qr_b, -qi_b), cat(cr_f, cr_f, cr_b, cr_b),
                            cat(-qi_f, -qr_f, -qi_b, -qr_b), cat(ci_f, ci_f, ci_b, ci_b))

    def impulse(d, lags):
        pr, pi = (v[d][lags].transpose(1, 0, 2) for v in (pw_re, pw_im))
        z = (cat(pr, -pi)[:, :, None, :] * cat(c_re[d], c_re[d])[:, None, :, :]
             + cat(-pi, -pr)[:, :, None, :] * cat(c_im[d], c_im[d])[:, None, :, :])
        b = cat(bb_re[d].transpose(0, 2, 1), bb_im[d].transpose(0, 2, 1))
        return jnp.einsum('gcq,gmq->gcm', b, z.reshape(N_GROUPS, tc, 2 * STATE), precision=lax.Precision.HIGHEST)

    k_f, k_b = impulse(0, t_idx), impulse(1, t_len - 1 - t_idx)
    col = lax.broadcasted_iota(jnp.int32, (1, 1, tc), 2)
    toe = jnp.stack([
        jnp.where(col >= GROUP_CH * t, jnp.roll(k_f, GROUP_CH * t, axis=-1), 0.0)
        + jnp.where(col < GROUP_CH * (t + 1), jnp.roll(k_b, -GROUP_CH * (t_len - 1 - t), axis=-1), 0.0)
        for t in range(t_len)], axis=1)
    eye = jnp.eye(PAIR, dtype=F32)
    w_intra = (toe.reshape(n_pairs, PAIR, tc, 1, tc) * eye[None, :, None, :, None]).reshape(
        N_SLABS, N_PAIRS, PAIR_COLS, PAIR_COLS).astype(BF16)

    def pair_state(x_re, x_im):
        blocks = [v.reshape(N_SLABS, N_PAIRS, 1, PAIR * STATE) for v in (x_re, x_im)]
        return jnp.concatenate(blocks, axis=2).reshape(N_SLABS, N_STATE_BLOCKS * LANES)

    lam_t = jnp.stack([pair_state(pw_re[0, t_len], pw_im[0, t_len]),
                       pair_state(pw_re[1, t_len], pw_im[1, t_len])], axis=1)
    d = jnp.broadcast_to(d_skip.astype(F32).reshape(N_SLABS, N_PAIRS, PAIR, 1, GROUP_CH),
                         (N_SLABS, N_PAIRS, PAIR, t_len, GROUP_CH)).reshape(N_SLABS, N_PAIRS, 1, PAIR_COLS)
    return w_in, w_intra, jnp.swapaxes(w_out_t, 2, 3), lam_t, d


def _s5_scan(u_slabs, seq, prep):
    w_in, w_intra, w_out, lam_t, d = prep
    n_tok = u_slabs.shape[1]
    nc = seq // S5_CHUNK
    n_seq = n_tok // seq
    bb = max(1, min(n_seq, 1024 // nc))
    assert n_seq % bb == 0 and seq % S5_CHUNK == 0
    rows = bb * nc
    rt = min(128, nc)
    rs = min(64, rt)
    assert nc % rt == 0 and rt % rs == 0
    zp = rs + 8
    pitch = nc + 4 if nc % 8 == 0 else nc
    pair_w = pl.BlockSpec((1, N_PAIRS, PAIR_COLS, PAIR_COLS), lambda s, r: (s, 0, 0, 0))
    tok_spec = pl.BlockSpec((1, rows * S5_CHUNK, LANES), lambda s, r: (s, r, 0))
    state = pltpu.VMEM((N_STATE_BLOCKS, bb * pitch, LANES), F32)
    return pl.pallas_call(
        functools.partial(_s5_kernel, bb=bb, nc=nc, rt=rt, rs=rs, zp=zp, pitch=pitch),
        grid=(N_SLABS, n_seq // bb),
        in_specs=[
            tok_spec, pair_w, pair_w, pair_w,
            pl.BlockSpec((1, 2, N_STATE_BLOCKS * LANES), lambda s, r: (s, 0, 0)),
            pl.BlockSpec((1, N_PAIRS, 1, PAIR_COLS), lambda s, r: (s, 0, 0, 0)),
        ],
        out_specs=tok_spec,
        out_shape=jax.ShapeDtypeStruct(u_slabs.shape, BF16),
        scratch_shapes=[pltpu.VMEM((N_PAIRS, rows, PAIR_COLS), BF16),
                        pltpu.VMEM((2, N_PAIRS, rt, PAIR_COLS), BF16),
                        pltpu.VMEM((STAGE_SLOTS, rs * S5_CHUNK, LANES), F32),
                        pltpu.VMEM((STAGE_SLOTS, S5_CHUNK * zp, LANES), F32), state, state],
        compiler_params=_cparams(2),
        name="s5_scan",
    )(u_slabs, w_in, w_intra, w_out, lam_t, d)


def _glu_kernel(g_ref, x_ref, w_ref, b_ref, o_ref):
    for rows in _row_parts(x_ref.shape[0]):
        g = jnp.concatenate([g_ref[s, rows, :] for s in range(N_SLABS)], axis=-1)
        h = jnp.dot(g, w_ref[...], preferred_element_type=F32) + b_ref[...]
        o_ref[rows, :] = x_ref[rows, :] + h[:, :D_MODEL] * jax.nn.sigmoid(h[:, D_MODEL:])


def _glu_residual(g_slabs, x2, w, b):
    n_tok = x2.shape[0]
    tt = _token_tile(n_tok, BIG_TILE)
    return pl.pallas_call(
        _glu_kernel,
        grid=(n_tok // tt,),
        in_specs=[
            pl.BlockSpec((N_SLABS, tt, LANES), lambda t: (0, t, 0)),
            pl.BlockSpec((tt, D_MODEL), lambda t: (t, 0)),
            _const_spec((D_MODEL, 2 * D_MODEL)),
            _const_spec((1, 2 * D_MODEL)),
        ],
        out_specs=pl.BlockSpec((tt, D_MODEL), lambda t: (t, 0)),
        out_shape=jax.ShapeDtypeStruct((n_tok, D_MODEL), F32),
        compiler_params=_cparams(1),
        name="s5_glu",
    )(g_slabs, x2, w, b.reshape(1, 2 * D_MODEL))


def _conv_in_kernel(x_ref, g_ref, w_ref, b_ref, o_ref):
    for rows in _row_parts(x_ref.shape[0]):
        xn = _rms(x_ref[rows, :], g_ref[...]).astype(BF16)
        h = jnp.dot(xn, w_ref[...], preferred_element_type=F32) + b_ref[...]
        o_ref[rows, :] = (h[:, :D_MODEL] * jax.nn.sigmoid(h[:, D_MODEL:])).astype(BF16)


def _conv_in(x2, g, w, b):
    n_tok = x2.shape[0]
    tt = _token_tile(n_tok, BIG_TILE)
    return pl.pallas_call(
        _conv_in_kernel,
        grid=(n_tok // tt,),
        in_specs=[
            pl.BlockSpec((tt, D_MODEL), lambda t: (t, 0)),
            _const_spec((1, D_MODEL)),
            _const_spec((D_MODEL, 2 * D_MODEL)),
            _const_spec((1, 2 * D_MODEL)),
        ],
        out_specs=pl.BlockSpec((tt, D_MODEL), lambda t: (t, 0)),
        out_shape=jax.ShapeDtypeStruct((n_tok, D_MODEL), BF16),
        compiler_params=_cparams(1),
        name="conv_in",
    )(x2, g.reshape(1, D_MODEL), w, b.reshape(1, 2 * D_MODEL))


def _conv_out_kernel(u_ref, up_ref, un_ref, x_ref, wdw_ref, bdw_ref, lg_ref, lb_ref, w_ref, b_ref,
                     o_ref, buf_ref, cv_ref, *, tt, rb):
    t = pl.program_id(1)
    last = pl.num_programs(1) - 1
    prev = jnp.where(t > 0, up_ref[0].astype(F32), 0.0)
    cur = u_ref[0].astype(F32)
    nxt = jnp.where(t < last, un_ref[0].astype(F32), 0.0)
    for s in range(N_SLABS):
        sl = slice(s * LANES, (s + 1) * LANES)
        buf_ref[s, 0:HALO, :] = prev[:, sl]
        buf_ref[s, HALO:HALO + tt, :] = cur[:, sl]
        buf_ref[s, HALO + tt:, :] = nxt[:, sl]
    off = HALO - CONV_PAD
    for s in range(N_SLABS):
        sl = slice(s * LANES, (s + 1) * LANES)
        for r0 in range(0, tt, rb):
            acc = jnp.broadcast_to(bdw_ref[:, sl], (rb, LANES))
            for k in range(CONV_WIDTH):
                acc = acc + wdw_ref[k:k + 1, sl] * buf_ref[s, r0 + off + k:r0 + off + k + rb, :]
            cv_ref[r0:r0 + rb, sl] = acc
    h = cv_ref[...]
    mu = jnp.mean(h, axis=-1, keepdims=True)
    hc = h - mu
    var = jnp.mean(hc * hc, axis=-1, keepdims=True)
    y = hc * lax.rsqrt(var + EPS) * lg_ref[...] + lb_ref[...]
    y = jax.nn.silu(y).astype(BF16)
    o_ref[0] = x_ref[0] + jnp.dot(y, w_ref[...], preferred_element_type=F32) + b_ref[...]


def _conv_out(u, x, w_dw, b_dw, ln_g, ln_b, w_out, b_out):
    bsz, seq, _ = x.shape
    tt = _token_tile(seq)
    assert tt % HALO == 0
    rb = min(64, tt)
    n_halo = seq // HALO
    per = tt // HALO
    row = lambda v: v.reshape(1, D_MODEL)
    return pl.pallas_call(
        functools.partial(_conv_out_kernel, tt=tt, rb=rb),
        grid=(bsz, seq // tt),
        in_specs=[
            pl.BlockSpec((1, tt, D_MODEL), lambda b, t: (b, t, 0)),
            pl.BlockSpec((1, HALO, D_MODEL), lambda b, t: (b, jnp.maximum(t * per - 1, 0), 0)),
            pl.BlockSpec((1, HALO, D_MODEL), lambda b, t: (b, jnp.minimum((t + 1) * per, n_halo - 1), 0)),
            pl.BlockSpec((1, tt, D_MODEL), lambda b, t: (b, t, 0)),
            _const_spec((CONV_WIDTH, D_MODEL)),
            _const_spec((1, D_MODEL)),
            _const_spec((1, D_MODEL)),
            _const_spec((1, D_MODEL)),
            _const_spec((D_MODEL, D_MODEL)),
            _const_spec((1, D_MODEL)),
        ],
        out_specs=pl.BlockSpec((1, tt, D_MODEL), lambda b, t: (b, t, 0)),
        out_shape=jax.ShapeDtypeStruct(x.shape, F32),
        scratch_shapes=[pltpu.VMEM((N_SLABS, tt + 2 * HALO, LANES), F32), pltpu.VMEM((tt, D_MODEL), F32)],
        compiler_params=_cparams(2),
        name="conv_out",
    )(u, u, u, x, w_dw, row(b_dw), row(ln_g), row(ln_b), w_out, row(b_out))


def _attn_kernel(x_ref, k_ref, v_ref, g_ref, wq_ref, wo_ref, o_ref):
    x = x_ref[0]
    xn = _rms(x, g_ref[...]).astype(BF16)
    q = jnp.dot(xn, wq_ref[...], preferred_element_type=F32)
    q = (q * (XHEAD_DIM ** -0.5)).astype(BF16)
    heads = []
    for h in range(N_XHEADS):
        sl = slice(h * XHEAD_DIM, (h + 1) * XHEAD_DIM)
        s = lax.dot_general(q[:, sl], k_ref[0, 0, :, sl], (((1,), (1,)), ((), ())),
                            preferred_element_type=F32)
        e = jnp.exp(s - jnp.max(s, axis=-1, keepdims=True))
        p = (e / jnp.sum(e, axis=-1, keepdims=True)).astype(BF16)
        heads.append(jnp.dot(p, v_ref[0, 0, :, sl], preferred_element_type=F32).astype(BF16))
    o = jnp.concatenate(heads, axis=-1)
    o_ref[0] = x + jnp.dot(o, wo_ref[...], preferred_element_type=F32)


def _cross_attention(x, k_all, v_all, layer, g, wq, wo):
    bsz, seq, _ = x.shape
    n_mem = k_all.shape[2]
    tt = _token_tile(seq, BIG_TILE)
    kv_spec = pl.BlockSpec((1, 1, n_mem, D_MODEL), lambda b, t: (layer, b, 0, 0))
    return pl.pallas_call(
        _attn_kernel,
        grid=(bsz, seq // tt),
        in_specs=[
            pl.BlockSpec((1, tt, D_MODEL), lambda b, t: (b, t, 0)),
            kv_spec, kv_spec,
            _const_spec((1, D_MODEL)),
            _const_spec((D_MODEL, D_MODEL)),
            _const_spec((D_MODEL, D_MODEL)),
        ],
        out_specs=pl.BlockSpec((1, tt, D_MODEL), lambda b, t: (b, t, 0)),
        out_shape=jax.ShapeDtypeStruct(x.shape, F32),
        compiler_params=_cparams(2),
        name="cross_attn",
    )(x, k_all, v_all, g.reshape(1, D_MODEL), wq, wo)


def _mlp_kernel(x_ref, g_ref, win_ref, wout_ref, gn_ref, o_ref, *rest, mode):
    x = x_ref[...]
    xn = _rms(x, g_ref[...]).astype(BF16)
    acc = x
    for c in range(D_FF // D_MODEL):
        sl = slice(c * D_MODEL, (c + 1) * D_MODEL)
        h = jnp.dot(xn, win_ref[:, sl], preferred_element_type=F32)
        h = jnp.square(jnp.maximum(h, 0.0)).astype(BF16)
        acc = acc + jnp.dot(h, wout_ref[sl, :], preferred_element_type=F32)
    if mode == "final":
        o_ref[...] = _rms(acc, gn_ref[...])
        return
    o_ref[...] = acc
    if mode == "s5_next":
        _store_rms_slabs(acc, gn_ref[...], rest[0])


def _mlp(x2, g, w_in, w_out, g_next, mode):
    n_tok = x2.shape[0]
    tt = _token_tile(n_tok, BIG_TILE)
    out_specs = [pl.BlockSpec((tt, D_MODEL), lambda t: (t, 0))]
    out_shape = [jax.ShapeDtypeStruct((n_tok, D_MODEL), F32)]
    if mode == "s5_next":
        out_specs.append(pl.BlockSpec((N_SLABS, tt, LANES), lambda t: (0, t, 0)))
        out_shape.append(jax.ShapeDtypeStruct((N_SLABS, n_tok, LANES), BF16))
    return pl.pallas_call(
        functools.partial(_mlp_kernel, mode=mode),
        grid=(n_tok // tt,),
        in_specs=[
            pl.BlockSpec((tt, D_MODEL), lambda t: (t, 0)),
            _const_spec((1, D_MODEL)),
            _const_spec((D_MODEL, D_FF)),
            _const_spec((D_FF, D_MODEL)),
            _const_spec((1, D_MODEL)),
        ],
        out_specs=out_specs,
        out_shape=out_shape,
        compiler_params=_cparams(1),
        name="mlp_" + mode,
    )(x2, g.reshape(1, D_MODEL), w_in, w_out, g_next.reshape(1, D_MODEL))


def _trunk(x, mem, p, s5_preps):
    bsz, seq, _ = x.shape
    depth = p['attn_w_q'].shape[0]
    k_all, v_all = _kv_all_layers(mem, p['norm_mem'], p['attn_w_kv'])
    flat = lambda a: a.reshape(bsz * seq, D_MODEL)
    u = None
    for i in range(depth):
        j = i // 2
        if i % 2 == 0:
            if u is None:
                u = _rms_to_slabs(flat(x), p['norm_mix'][i])
            g = _s5_scan(u, seq, s5_preps[j])
            x = _glu_residual(g, flat(x), p['ssm_w_glu'][j], p['ssm_b_glu'][j]).reshape(x.shape)
        else:
            u = _conv_in(flat(x), p['norm_mix'][i], p['conv_w_in'][j], p['conv_b_in'][j])
            x = _conv_out(u.reshape(x.shape), x, p['conv_w_dw'][j], p['conv_b_dw'][j],
                          p['conv_ln_g'][j], p['conv_ln_b'][j], p['conv_w_out'][j], p['conv_b_out'][j])
        x = _cross_attention(x, k_all, v_all, i, p['norm_xq'][i], p['attn_w_q'][i], p['attn_w_o'][i])
        if i == depth - 1:
            mode, g_next = "final", p['norm_final']
        elif (i + 1) % 2 == 0:
            mode, g_next = "s5_next", p['norm_mix'][i + 1]
        else:
            mode, g_next = "plain", p['norm_final']
        out = _mlp(flat(x), p['norm_ffn'][i], p['mlp_w_in'][i], p['mlp_w_out'][i], g_next, mode)
        x = out[0].reshape(x.shape)
        u = out[1] if mode == "s5_next" else None
    return x


def kernel(x_prompt, x_sample, mem_prompt, mem_sample, norm_mix, norm_xq, norm_mem, norm_ffn, norm_final, ssm_a_re, ssm_a_im, ssm_log_step, ssm_b_re, ssm_b_im, ssm_c_re, ssm_c_im, ssm_d, ssm_w_glu, ssm_b_glu, conv_w_in, conv_b_in, conv_w_dw, conv_b_dw, conv_ln_g, conv_ln_b, conv_w_out, conv_b_out, attn_w_q, attn_w_kv, attn_w_o, mlp_w_in, mlp_w_out):
    p = dict(norm_mix=norm_mix, norm_xq=norm_xq, norm_mem=norm_mem, norm_ffn=norm_ffn,
             norm_final=norm_final, ssm_b_glu=ssm_b_glu,
             conv_b_in=conv_b_in, conv_w_dw=conv_w_dw, conv_b_dw=conv_b_dw,
             conv_ln_g=conv_ln_g, conv_ln_b=conv_ln_b, conv_b_out=conv_b_out)
    for name, w in (('ssm_w_glu', ssm_w_glu), ('conv_w_in', conv_w_in), ('conv_w_out', conv_w_out),
                    ('attn_w_q', attn_w_q), ('attn_w_kv', attn_w_kv), ('attn_w_o', attn_w_o),
                    ('mlp_w_in', mlp_w_in), ('mlp_w_out', mlp_w_out)):
        p[name] = w.astype(BF16)
    stacked = jax.vmap(_s5_prepare)(ssm_a_re, ssm_a_im, ssm_log_step, ssm_b_re, ssm_b_im,
                                    ssm_c_re, ssm_c_im, ssm_d)
    s5_preps = [tuple(a[j] for a in stacked) for j in range(ssm_a_re.shape[0])]
    y_prompt = _trunk(x_prompt, mem_prompt, p, s5_preps)
    y_sample = _trunk(x_sample, mem_sample, p, s5_preps)
    return (y_prompt, y_sample)
```

```python
import functools
import math

import jax
import jax.numpy as jnp
from jax import lax
from jax.experimental import pallas as pl
from jax.experimental.pallas import tpu as pltpu

D_MODEL = 1024
GROUP_CH = 16
N_GROUPS = D_MODEL // GROUP_CH
STATE = 64
CONV_WIDTH = 31
CONV_PAD = (CONV_WIDTH - 1) // 2
N_XHEADS = 4
XHEAD_DIM = D_MODEL // N_XHEADS
D_FF = 4 * D_MODEL
EPS = 1e-6

LANES = 128
N_SLABS = D_MODEL // LANES
GROUPS_PER_SLAB = LANES // GROUP_CH
S5_CHUNK = 16
N_HALF = S5_CHUNK // GROUPS_PER_SLAB
PAIR = 2
N_PAIRS = GROUPS_PER_SLAB // PAIR
PAIR_COLS = PAIR * S5_CHUNK * GROUP_CH
PAIR_STATE = 4 * PAIR * STATE
N_STATE_BLOCKS = 2 * N_PAIRS
STAGE_SLOTS = 4
STAGE_PITCH = S5_CHUNK + 8
HALO = 16
VMEM_LIMIT = 56 * 1024 * 1024

F32 = jnp.float32
BF16 = jnp.bfloat16


def _cparams(n_axes):
    return pltpu.CompilerParams(
        dimension_semantics=("arbitrary",) * n_axes, vmem_limit_bytes=VMEM_LIMIT)


def _const_spec(shape):
    nd = len(shape)
    return pl.BlockSpec(shape, lambda *_: (0,) * nd, pipeline_mode=pl.Buffered(1))


def _rms(x, g):
    return x * lax.rsqrt(jnp.mean(x * x, axis=-1, keepdims=True) + EPS) * g


def _token_tile(n, cap=512):
    t = min(cap, n)
    assert n % t == 0
    return t


BIG_TILE = 1024


def _row_parts(n):
    parts = 2 if n % 32 == 0 else 1
    return [slice(i * n // parts, (i + 1) * n // parts) for i in range(parts)]


def _kv_kernel(mem_ref, g_ref, w_ref, k_ref, v_ref):
    mn = _rms(mem_ref[0], g_ref[0]).astype(BF16)
    kv = jnp.dot(mn, w_ref[0], preferred_element_type=F32)
    k_ref[0, 0] = kv[:, :D_MODEL].astype(BF16)
    v_ref[0, 0] = kv[:, D_MODEL:].astype(BF16)


def _kv_all_layers(mem, norm_mem, w_kv):
    bsz, n_mem, _ = mem.shape
    depth = w_kv.shape[0]
    out = jax.ShapeDtypeStruct((depth, bsz, n_mem, D_MODEL), BF16)
    return pl.pallas_call(
        _kv_kernel,
        grid=(depth, bsz),
        in_specs=[
            pl.BlockSpec((1, n_mem, D_MODEL), lambda i, b: (b, 0, 0)),
            pl.BlockSpec((1, 1, D_MODEL), lambda i, b: (i, 0, 0)),
            pl.BlockSpec((1, D_MODEL, 2 * D_MODEL), lambda i, b: (i, 0, 0)),
        ],
        out_specs=[
            pl.BlockSpec((1, 1, n_mem, D_MODEL), lambda i, b: (i, b, 0, 0)),
            pl.BlockSpec((1, 1, n_mem, D_MODEL), lambda i, b: (i, b, 0, 0)),
        ],
        out_shape=[out, out],
        compiler_params=_cparams(2),
        name="kv_proj",
    )(mem, norm_mem.reshape(depth, 1, D_MODEL), w_kv)


def _store_rms_slabs(x, g, o_ref):
    xn = _rms(x, g).astype(BF16)
    for s in range(N_SLABS):
        o_ref[s] = xn[:, s * LANES:(s + 1) * LANES]


def _rms_slab_kernel(x_ref, g_ref, o_ref):
    _store_rms_slabs(x_ref[...], g_ref[...], o_ref)


def _rms_to_slabs(x2, g):
    n_tok = x2.shape[0]
    tt = _token_tile(n_tok)
    return pl.pallas_call(
        _rms_slab_kernel,
        grid=(n_tok // tt,),
        in_specs=[pl.BlockSpec((tt, D_MODEL), lambda t: (t, 0)), _const_spec((1, D_MODEL))],
        out_specs=pl.BlockSpec((N_SLABS, tt, LANES), lambda t: (0, t, 0)),
        out_shape=jax.ShapeDtypeStruct((N_SLABS, n_tok, LANES), BF16),
        compiler_params=_cparams(1),
        name="rms_slabs",
    )(x2, g.reshape(1, D_MODEL))


def _lane_block_transpose(vs):
    vs = list(vs)
    blk = lax.broadcasted_iota(jnp.int32, vs[0].shape, 1) // GROUP_CH
    d = 1
    while d < len(vs):
        upper = (blk & d) != 0
        for i in range(len(vs)):
            if i & d:
                continue
            a, b = vs[i], vs[i | d]
            vs[i] = jnp.where(upper, pltpu.roll(b, d * GROUP_CH, 1), a)
            vs[i | d] = jnp.where(upper, b, pltpu.roll(a, LANES - d * GROUP_CH, 1))
        d *= 2
    return vs


def _s5_kernel(u_ref, win_ref, wi_ref, wo_ref, lam_ref, d_ref, o_ref,
               ulhs_ref, g_ref, stage_ref, zs_ref, sf_ref, sb_ref, *, bb, nc, rt, rs, zp, pitch):
    tiles = [(b * nc + k0, b * pitch + k0) for b in range(bb) for k0 in range(0, nc, rt)]
    as_words = lambda v: pltpu.bitcast(v, jnp.int32)
    as_bf16 = lambda v: pltpu.bitcast(v, BF16)

    n_sub = 0
    for r0, p0 in tiles:
        for q0 in range(0, rt, rs):
            tok0 = (r0 + q0) * S5_CHUNK
            stage = stage_ref.at[n_sub % stage_ref.shape[0]]
            n_sub += 1
            tok = u_ref[0, tok0:tok0 + rs * S5_CHUNK, :].astype(F32)
            for k in range(rs):
                stage[k * STAGE_PITCH:k * STAGE_PITCH + S5_CHUNK, :] = tok[k * S5_CHUNK:(k + 1) * S5_CHUNK, :]
            steps = [as_words(stage[pl.ds(t, rs, stride=STAGE_PITCH), :].astype(BF16))
                     for t in range(S5_CHUNK)]
            runs = [_lane_block_transpose(steps[h * GROUPS_PER_SLAB:(h + 1) * GROUPS_PER_SLAB])
                    for h in range(N_HALF)]
            for pp in range(N_PAIRS):
                ulhs_ref[pp, r0 + q0:r0 + q0 + rs, :] = jnp.concatenate(
                    [as_bf16(runs[h][PAIR * pp + g]) for g in range(PAIR) for h in range(N_HALF)], axis=-1)
        for pp in range(N_PAIRS):
            s = jnp.dot(ulhs_ref[pp, r0:r0 + rt, :], win_ref[0, pp], preferred_element_type=F32)
            for c in range(2):
                sf_ref[2 * pp + c, p0:p0 + rt, :] = s[:, c * LANES:(c + 1) * LANES]
                sb_ref[2 * pp + c, p0:p0 + rt, :] = s[:, (2 + c) * LANES:(3 + c) * LANES]

    lam = lam_ref[0]

    def rows_of_chunk(k):
        return pl.ds(k, bb, stride=pitch) if bb > 1 else pl.ds(k, 1)

    def advance(s_ref, lam_row, idx, carry):
        new = [None] * N_STATE_BLOCKS
        for pp in range(N_PAIRS):
            re, im = 2 * pp, 2 * pp + 1
            l_re = lam_row[:, re * LANES:(re + 1) * LANES]
            l_im = lam_row[:, im * LANES:(im + 1) * LANES]
            s_re, s_im = s_ref[re, idx, :], s_ref[im, idx, :]
            c_re, c_im = carry[re], carry[im]
            s_ref[re, idx, :] = c_re
            s_ref[im, idx, :] = c_im
            new[re] = l_re * c_re - l_im * c_im + s_re
            new[im] = l_re * c_im + l_im * c_re + s_im
        return tuple(new)

    def body(k, carry):
        cf, cb = carry
        return (advance(sf_ref, lam[0:1], rows_of_chunk(k), cf),
                advance(sb_ref, lam[1:2], rows_of_chunk(nc - 1 - k), cb))

    zero = tuple(jnp.zeros((bb, LANES), F32) for _ in range(N_STATE_BLOCKS))
    lax.fori_loop(0, nc, body, (zero, zero))

    n_sub = 0
    for n_tile, (r0, p0) in enumerate(tiles):
        g_tile = g_ref.at[n_tile % g_ref.shape[0]]
        for pp in range(N_PAIRS):
            u = ulhs_ref[pp, r0:r0 + rt, :]
            entering = jnp.concatenate([s_ref[2 * pp + c, p0:p0 + rt, :]
                                        for s_ref in (sf_ref, sb_ref) for c in range(2)], axis=-1).astype(BF16)
            y = jnp.dot(u, wi_ref[0, pp], preferred_element_type=F32)
            y = y + jnp.dot(entering, wo_ref[0, pp], preferred_element_type=F32)
            g_tile[pp, 0:rt, :] = jax.nn.gelu(y + d_ref[0, pp] * u.astype(F32)).astype(BF16)
        for q0 in range(0, rt, rs):
            zs = zs_ref.at[n_sub % zs_ref.shape[0]]
            n_sub += 1
            for h in range(N_HALF):
                run = _lane_block_transpose([
                    as_words(g_tile[q // PAIR, q0:q0 + rs,
                                    ((q % PAIR) * N_HALF + h) * LANES:((q % PAIR) * N_HALF + h + 1) * LANES])
                    for q in range(GROUPS_PER_SLAB)])
                for i in range(GROUPS_PER_SLAB):
                    z0 = (h * GROUPS_PER_SLAB + i) * zp
                    zs[z0:z0 + rs, :] = as_bf16(run[i]).astype(F32)
            tok0 = (r0 + q0) * S5_CHUNK
            o_ref[0, tok0:tok0 + rs * S5_CHUNK, :] = jnp.concatenate(
                [zs[pl.ds(k, S5_CHUNK, stride=zp), :] for k in range(rs)], axis=0).astype(BF16)


def _s5_prepare(a_re, a_im, log_step, b_re, b_im, c_re, c_im, d_skip):
    t_len = S5_CHUNK
    step = jnp.exp(log_step.astype(F32))[..., None]
    ar, ai = a_re.astype(F32) * step, a_im.astype(F32) * step
    n = jnp.arange(t_len + 1, dtype=F32)[None, :, None, None]
    mag = jnp.exp(ar[:, None] * n)
    pw_re, pw_im = mag * jnp.cos(ai[:, None] * n), mag * jnp.sin(ai[:, None] * n)
    num_re, num_im = pw_re[:, 1] - 1.0, pw_im[:, 1]
    den = a_re * a_re + a_im * a_im
    z_re = (num_re * a_re + num_im * a_im) / den
    z_im = (num_im * a_re - num_re * a_im) / den
    bb_re = z_re[..., None] * b_re - z_im[..., None] * b_im
    bb_im = z_re[..., None] * b_im + z_im[..., None] * b_re

    t_idx = jnp.arange(t_len)
    n_pairs = N_GROUPS // PAIR
    tc = t_len * GROUP_CH
    cat = lambda *xs: jnp.concatenate(xs, axis=-1)

    def powers(d, idx):
        return tuple(v[d][idx].reshape(t_len, n_pairs, LANES).transpose(1, 0, 2) for v in (pw_re, pw_im))

    def b_rows(x):
        return x.reshape(n_pairs, PAIR, STATE, GROUP_CH).transpose(0, 3, 1, 2).reshape(n_pairs, GROUP_CH, LANES)

    def c_rows(x):
        return x.reshape(n_pairs, PAIR, GROUP_CH, STATE).transpose(0, 2, 1, 3).reshape(n_pairs, GROUP_CH, LANES)

    def pair_operator(p1, r1, p2, r2):
        x = p1[:, :, None, :] * r1[:, None, :, :] + p2[:, :, None, :] * r2[:, None, :, :]
        shape = (1, PAIR, 1, 1, PAIR_STATE)
        same = lax.broadcasted_iota(jnp.int32, shape, 1) == (lax.broadcasted_iota(jnp.int32, shape, 4) // STATE) % PAIR
        return jnp.where(same, x[:, None], 0.0).reshape(N_SLABS, N_PAIRS, PAIR_COLS, PAIR_STATE).astype(BF16)

    (pr_f, pi_f), (pr_b, pi_b) = powers(0, t_len - 1 - t_idx), powers(1, t_idx)
    br_f, bi_f, br_b, bi_b = b_rows(bb_re[0]), b_rows(bb_im[0]), b_rows(bb_re[1]), b_rows(bb_im[1])
    w_in = pair_operator(cat(pr_f, pr_f, pr_b, pr_b), cat(br_f, bi_f, br_b, bi_b),
                         cat(-pi_f, pi_f, -pi_b, pi_b), cat(bi_f, br_f, bi_b, br_b))

    (qr_f, qi_f), (qr_b, qi_b) = powers(0, t_idx + 1), powers(1, t_len - t_idx)
    cr_f, ci_f, cr_b, ci_b = c_rows(c_re[0]), c_rows(c_im[0]), c_rows(c_re[1]), c_rows(c_im[1])
    w_out_t = pair_operator(cat(qr_f, -qi_f, qr_b, -qi_b), cat(cr_f, cr_f, cr_b, cr_b),
                            cat(-qi_f, -qr_f, -qi_b, -qr_b), cat(ci_f, ci_f, ci_b, ci_b))

    def impulse(d, lags):
        pr, pi = (v[d][lags].transpose(1, 0, 2) for v in (pw_re, pw_im))
        z = (cat(pr, -pi)[:, :, None, :] * cat(c_re[d], c_re[d])[:, None, :, :]
             + cat(-pi, -pr)[:, :, None, :] * cat(c_im[d], c_im[d])[:, None, :, :])
        b = cat(bb_re[d].transpose(0, 2, 1), bb_im[d].transpose(0, 2, 1))
        return jnp.einsum('gcq,gmq->gcm', b, z.reshape(N_GROUPS, tc, 2 * STATE), precision=lax.Precision.HIGHEST)

    k_f, k_b = impulse(0, t_idx), impulse(1, t_len - 1 - t_idx)
    col = lax.broadcasted_iota(jnp.int32, (1, 1, tc), 2)
    toe = jnp.stack([
        jnp.where(col >= GROUP_CH * t, jnp.roll(k_f, GROUP_CH * t, axis=-1), 0.0)
        + jnp.where(col < GROUP_CH * (t + 1), jnp.roll(k_b, -GROUP_CH * (t_len - 1 - t), axis=-1), 0.0)
        for t in range(t_len)], axis=1)
    eye = jnp.eye(PAIR, dtype=F32)
    w_intra = (toe.reshape(n_pairs, PAIR, tc, 1, tc) * eye[None, :, None, :, None]).reshape(
        N_SLABS, N_PAIRS, PAIR_COLS, PAIR_COLS).astype(BF16)

    def pair_state(x_re, x_im):
        blocks = [v.reshape(N_SLABS, N_PAIRS, 1, PAIR * STATE) for v in (x_re, x_im)]
        return jnp.concatenate(blocks, axis=2).reshape(N_SLABS, N_STATE_BLOCKS * LANES)

    lam_t = jnp.stack([pair_state(pw_re[0, t_len], pw_im[0, t_len]),
                       pair_state(pw_re[1, t_len], pw_im[1, t_len])], axis=1)
    d = jnp.broadcast_to(d_skip.astype(F32).reshape(N_SLABS, N_PAIRS, PAIR, 1, GROUP_CH),
                         (N_SLABS, N_PAIRS, PAIR, t_len, GROUP_CH)).reshape(N_SLABS, N_PAIRS, 1, PAIR_COLS)
    return w_in, w_intra, jnp.swapaxes(w_out_t, 2, 3), lam_t, d


def _s5_scan(u_slabs, seq, prep):
    w_in, w_intra, w_out, lam_t, d = prep
    n_tok = u_slabs.shape[1]
    nc = seq // S5_CHUNK
    n_seq = n_tok // seq
    bb = max(1, min(n_seq, 1024 // nc))
    assert n_seq % bb == 0 and seq % S5_CHUNK == 0
    rows = bb * nc
    rt = min(128, nc)
    rs = min(64, rt)
    assert nc % rt == 0 and rt % rs == 0
    zp = rs + 8
    pitch = nc + 4 if nc % 8 == 0 else nc
    pair_w = pl.BlockSpec((1, N_PAIRS, PAIR_COLS, PAIR_COLS), lambda s, r: (s, 0, 0, 0))
    tok_spec = pl.BlockSpec((1, rows * S5_CHUNK, LANES), lambda s, r: (s, r, 0))
    state = pltpu.VMEM((N_STATE_BLOCKS, bb * pitch, LANES), F32)
    return pl.pallas_call(
        functools.partial(_s5_kernel, bb=bb, nc=nc, rt=rt, rs=rs, zp=zp, pitch=pitch),
        grid=(N_SLABS, n_seq // bb),
        in_specs=[
            tok_spec, pair_w, pair_w, pair_w,
            pl.BlockSpec((1, 2, N_STATE_BLOCKS * LANES), lambda s, r: (s, 0, 0)),
            pl.BlockSpec((1, N_PAIRS, 1, PAIR_COLS), lambda s, r: (s, 0, 0, 0)),
        ],
        out_specs=tok_spec,
        out_shape=jax.ShapeDtypeStruct(u_slabs.shape, BF16),
        scratch_shapes=[pltpu.VMEM((N_PAIRS, rows, PAIR_COLS), BF16),
                        pltpu.VMEM((2, N_PAIRS, rt, PAIR_COLS), BF16),
                        pltpu.VMEM((STAGE_SLOTS, rs * STAGE_PITCH, LANES), F32),
                        pltpu.VMEM((STAGE_SLOTS, S5_CHUNK * zp, LANES), F32), state, state],
        compiler_params=_cparams(2),
        name="s5_scan",
    )(u_slabs, w_in, w_intra, w_out, lam_t, d)


def _glu_kernel(g_ref, x_ref, w_ref, b_ref, o_ref):
    for rows in _row_parts(x_ref.shape[0]):
        g = jnp.concatenate([g_ref[s, rows, :] for s in range(N_SLABS)], axis=-1)
        h = jnp.dot(g, w_ref[...], preferred_element_type=F32) + b_ref[...]
        o_ref[rows, :] = x_ref[rows, :] + h[:, :D_MODEL] * jax.nn.sigmoid(h[:, D_MODEL:])


def _glu_residual(g_slabs, x2, w, b):
    n_tok = x2.shape[0]
    tt = _token_tile(n_tok, BIG_TILE)
    return pl.pallas_call(
        _glu_kernel,
        grid=(n_tok // tt,),
        in_specs=[
            pl.BlockSpec((N_SLABS, tt, LANES), lambda t: (0, t, 0)),
            pl.BlockSpec((tt, D_MODEL), lambda t: (t, 0)),
            _const_spec((D_MODEL, 2 * D_MODEL)),
            _const_spec((1, 2 * D_MODEL)),
        ],
        out_specs=pl.BlockSpec((tt, D_MODEL), lambda t: (t, 0)),
        out_shape=jax.ShapeDtypeStruct((n_tok, D_MODEL), F32),
        compiler_params=_cparams(1),
        name="s5_glu",
    )(g_slabs, x2, w, b.reshape(1, 2 * D_MODEL))


def _conv_in_kernel(x_ref, g_ref, w_ref, b_ref, o_ref):
    for rows in _row_parts(x_ref.shape[0]):
        xn = _rms(x_ref[rows, :], g_ref[...]).astype(BF16)
        h = jnp.dot(xn, w_ref[...], preferred_element_type=F32) + b_ref[...]
        o_ref[rows, :] = (h[:, :D_MODEL] * jax.nn.sigmoid(h[:, D_MODEL:])).astype(BF16)


def _conv_in(x2, g, w, b):
    n_tok = x2.shape[0]
    tt = _token_tile(n_tok, BIG_TILE)
    return pl.pallas_call(
        _conv_in_kernel,
        grid=(n_tok // tt,),
        in_specs=[
            pl.BlockSpec((tt, D_MODEL), lambda t: (t, 0)),
            _const_spec((1, D_MODEL)),
            _const_spec((D_MODEL, 2 * D_MODEL)),
            _const_spec((1, 2 * D_MODEL)),
        ],
        out_specs=pl.BlockSpec((tt, D_MODEL), lambda t: (t, 0)),
        out_shape=jax.ShapeDtypeStruct((n_tok, D_MODEL), BF16),
        compiler_params=_cparams(1),
        name="conv_in",
    )(x2, g.reshape(1, D_MODEL), w, b.reshape(1, 2 * D_MODEL))


def _conv_out_kernel(u_ref, up_ref, un_ref, x_ref, wdw_ref, bdw_ref, lg_ref, lb_ref, w_ref, b_ref,
                     o_ref, buf_ref, cv_ref, *, tt, rb):
    t = pl.program_id(1)
    last = pl.num_programs(1) - 1
    prev = jnp.where(t > 0, up_ref[0].astype(F32), 0.0)
    cur = u_ref[0].astype(F32)
    nxt = jnp.where(t < last, un_ref[0].astype(F32), 0.0)
    for s in range(N_SLABS):
        sl = slice(s * LANES, (s + 1) * LANES)
        buf_ref[s, 0:HALO, :] = prev[:, sl]
        buf_ref[s, HALO:HALO + tt, :] = cur[:, sl]
        buf_ref[s, HALO + tt:, :] = nxt[:, sl]
    off = HALO - CONV_PAD
    for s in range(N_SLABS):
        sl = slice(s * LANES, (s + 1) * LANES)
        for r0 in range(0, tt, rb):
            acc = jnp.broadcast_to(bdw_ref[:, sl], (rb, LANES))
            for k in range(CONV_WIDTH):
                acc = acc + wdw_ref[k:k + 1, sl] * buf_ref[s, r0 + off + k:r0 + off + k + rb, :]
            cv_ref[r0:r0 + rb, sl] = acc
    h = cv_ref[...]
    mu = jnp.mean(h, axis=-1, keepdims=True)
    hc = h - mu
    var = jnp.mean(hc * hc, axis=-1, keepdims=True)
    y = hc * lax.rsqrt(var + EPS) * lg_ref[...] + lb_ref[...]
    y = jax.nn.silu(y).astype(BF16)
    o_ref[0] = x_ref[0] + jnp.dot(y, w_ref[...], preferred_element_type=F32) + b_ref[...]


def _conv_out(u, x, w_dw, b_dw, ln_g, ln_b, w_out, b_out):
    bsz, seq, _ = x.shape
    tt = _token_tile(seq)
    assert tt % HALO == 0
    rb = min(128, tt)
    n_halo = seq // HALO
    per = tt // HALO
    row = lambda v: v.reshape(1, D_MODEL)
    return pl.pallas_call(
        functools.partial(_conv_out_kernel, tt=tt, rb=rb),
        grid=(bsz, seq // tt),
        in_specs=[
            pl.BlockSpec((1, tt, D_MODEL), lambda b, t: (b, t, 0)),
            pl.BlockSpec((1, HALO, D_MODEL), lambda b, t: (b, jnp.maximum(t * per - 1, 0), 0)),
            pl.BlockSpec((1, HALO, D_MODEL), lambda b, t: (b, jnp.minimum((t + 1) * per, n_halo - 1), 0)),
            pl.BlockSpec((1, tt, D_MODEL), lambda b, t: (b, t, 0)),
            _const_spec((CONV_WIDTH, D_MODEL)),
            _const_spec((1, D_MODEL)),
            _const_spec((1, D_MODEL)),
            _const_spec((1, D_MODEL)),
            _const_spec((D_MODEL, D_MODEL)),
            _const_spec((1, D_MODEL)),
        ],
        out_specs=pl.BlockSpec((1, tt, D_MODEL), lambda b, t: (b, t, 0)),
        out_shape=jax.ShapeDtypeStruct(x.shape, F32),
        scratch_shapes=[pltpu.VMEM((N_SLABS, tt + 2 * HALO, LANES), F32), pltpu.VMEM((tt, D_MODEL), F32)],
        compiler_params=_cparams(2),
        name="conv_out",
    )(u, u, u, x, w_dw, row(b_dw), row(ln_g), row(ln_b), w_out, row(b_out))


def _attn_kernel(x_ref, k_ref, v_ref, g_ref, wq_ref, wo_ref, o_ref):
    x = x_ref[0]
    xn = _rms(x, g_ref[...]).astype(BF16)
    q = jnp.dot(xn, wq_ref[...], preferred_element_type=F32)
    q = (q * (XHEAD_DIM ** -0.5)).astype(BF16)
    heads = []
    for h in range(N_XHEADS):
        sl = slice(h * XHEAD_DIM, (h + 1) * XHEAD_DIM)
        s = lax.dot_general(q[:, sl], k_ref[0, 0, :, sl], (((1,), (1,)), ((), ())),
                            preferred_element_type=F32)
        e = jnp.exp(s - jnp.max(s, axis=-1, keepdims=True))
        p = (e / jnp.sum(e, axis=-1, keepdims=True)).astype(BF16)
        heads.append(jnp.dot(p, v_ref[0, 0, :, sl], preferred_element_type=F32).astype(BF16))
    o = jnp.concatenate(heads, axis=-1)
    o_ref[0] = x + jnp.dot(o, wo_ref[...], preferred_element_type=F32)


def _cross_attention(x, k_all, v_all, layer, g, wq, wo):
    bsz, seq, _ = x.shape
    n_mem = k_all.shape[2]
    tt = _token_tile(seq, BIG_TILE)
    kv_spec = pl.BlockSpec((1, 1, n_mem, D_MODEL), lambda b, t: (layer, b, 0, 0))
    return pl.pallas_call(
        _attn_kernel,
        grid=(bsz, seq // tt),
        in_specs=[
            pl.BlockSpec((1, tt, D_MODEL), lambda b, t: (b, t, 0)),
            kv_spec, kv_spec,
            _const_spec((1, D_MODEL)),
            _const_spec((D_MODEL, D_MODEL)),
            _const_spec((D_MODEL, D_MODEL)),
        ],
        out_specs=pl.BlockSpec((1, tt, D_MODEL), lambda b, t: (b, t, 0)),
        out_shape=jax.ShapeDtypeStruct(x.shape, F32),
        compiler_params=_cparams(2),
        name="cross_attn",
    )(x, k_all, v_all, g.reshape(1, D_MODEL), wq, wo)


def _mlp_kernel(x_ref, g_ref, win_ref, wout_ref, gn_ref, o_ref, *rest, mode):
    x = x_ref[...]
    xn = _rms(x, g_ref[...]).astype(BF16)
    acc = x
    for c in range(D_FF // D_MODEL):
        sl = slice(c * D_MODEL, (c + 1) * D_MODEL)
        h = jnp.dot(xn, win_ref[:, sl], preferred_element_type=F32)
        h = jnp.square(jnp.maximum(h, 0.0)).astype(BF16)
        acc = acc + jnp.dot(h, wout_ref[sl, :], preferred_element_type=F32)
    if mode == "final":
        o_ref[...] = _rms(acc, gn_ref[...])
        return
    o_ref[...] = acc
    if mode == "s5_next":
        _store_rms_slabs(acc, gn_ref[...], rest[0])


def _mlp(x2, g, w_in, w_out, g_next, mode):
    n_tok = x2.shape[0]
    tt = _token_tile(n_tok, BIG_TILE)
    out_specs = [pl.BlockSpec((tt, D_MODEL), lambda t: (t, 0))]
    out_shape = [jax.ShapeDtypeStruct((n_tok, D_MODEL), F32)]
    if mode == "s5_next":
        out_specs.append(pl.BlockSpec((N_SLABS, tt, LANES), lambda t: (0, t, 0)))
        out_shape.append(jax.ShapeDtypeStruct((N_SLABS, n_tok, LANES), BF16))
    return pl.pallas_call(
        functools.partial(_mlp_kernel, mode=mode),
        grid=(n_tok // tt,),
        in_specs=[
            pl.BlockSpec((tt, D_MODEL), lambda t: (t, 0)),
            _const_spec((1, D_MODEL)),
            _const_spec((D_MODEL, D_FF)),
            _const_spec((D_FF, D_MODEL)),
            _const_spec((1, D_MODEL)),
        ],
        out_specs=out_specs,
        out_shape=out_shape,
        compiler_params=_cparams(1),
        name="mlp_" + mode,
    )(x2, g.reshape(1, D_MODEL), w_in, w_out, g_next.reshape(1, D_MODEL))


def _trunk(x, mem, p, s5_preps):
    bsz, seq, _ = x.shape
    depth = p['attn_w_q'].shape[0]
    k_all, v_all = _kv_all_layers(mem, p['norm_mem'], p['attn_w_kv'])
    flat = lambda a: a.reshape(bsz * seq, D_MODEL)
    u = None
    for i in range(depth):
        j = i // 2
        if i % 2 == 0:
            if u is None:
                u = _rms_to_slabs(flat(x), p['norm_mix'][i])
            g = _s5_scan(u, seq, s5_preps[j])
            x = _glu_residual(g, flat(x), p['ssm_w_glu'][j], p['ssm_b_glu'][j]).reshape(x.shape)
        else:
            u = _conv_in(flat(x), p['norm_mix'][i], p['conv_w_in'][j], p['conv_b_in'][j])
            x = _conv_out(u.reshape(x.shape), x, p['conv_w_dw'][j], p['conv_b_dw'][j],
                          p['conv_ln_g'][j], p['conv_ln_b'][j], p['conv_w_out'][j], p['conv_b_out'][j])
        x = _cross_attention(x, k_all, v_all, i, p['norm_xq'][i], p['attn_w_q'][i], p['attn_w_o'][i])
        if i == depth - 1:
            mode, g_next = "final", p['norm_final']
        elif (i + 1) % 2 == 0:
            mode, g_next = "s5_next", p['norm_mix'][i + 1]
        else:
            mode, g_next = "plain", p['norm_final']
        out = _mlp(flat(x), p['norm_ffn'][i], p['mlp_w_in'][i], p['mlp_w_out'][i], g_next, mode)
        x = out[0].reshape(x.shape)
        u = out[1] if mode == "s5_next" else None
    return x


def kernel(x_prompt, x_sample, mem_prompt, mem_sample, norm_mix, norm_xq, norm_mem, norm_ffn, norm_final, ssm_a_re, ssm_a_im, ssm_log_step, ssm_b_re, ssm_b_im, ssm_c_re, ssm_c_im, ssm_d, ssm_w_glu, ssm_b_glu, conv_w_in, conv_b_in, conv_w_dw, conv_b_dw, conv_ln_g, conv_ln_b, conv_w_out, conv_b_out, attn_w_q, attn_w_kv, attn_w_o, mlp_w_in, mlp_w_out):
    p = dict(norm_mix=norm_mix, norm_xq=norm_xq, norm_mem=norm_mem, norm_ffn=norm_ffn,
             norm_final=norm_final, ssm_b_glu=ssm_b_glu,
             conv_b_in=conv_b_in, conv_w_dw=conv_w_dw, conv_b_dw=conv_b_dw,
             conv_ln_g=conv_ln_g, conv_ln_b=conv_ln_b, conv_b_out=conv_b_out)
    for name, w in (('ssm_w_glu', ssm_w_glu), ('conv_w_in', conv_w_in), ('conv_w_out', conv_w_out),
                    ('attn_w_q', attn_w_q), ('attn_w_kv', attn_w_kv), ('attn_w_o', attn_w_o),
                    ('mlp_w_in', mlp_w_in), ('mlp_w_out', mlp_w_out)):
        p[name] = w.astype(BF16)
    stacked = jax.vmap(_s5_prepare)(ssm_a_re, ssm_a_im, ssm_log_step, ssm_b_re, ssm_b_im,
                                    ssm_c_re, ssm_c_im, ssm_d)
    s5_preps = [tuple(a[j] for a in stacked) for j in range(ssm_a_re.shape[0])]
    y_prompt = _trunk(x_prompt, mem_prompt, p, s5_preps)
    y_sample = _trunk(x_sample, mem_sample, p, s5_preps)
    return (y_prompt, y_sample)
```

```python
import functools
import math

import jax
import jax.numpy as jnp
from jax import lax
from jax.experimental import pallas as pl
from jax.experimental.pallas import tpu as pltpu

D_MODEL = 1024
GROUP_CH = 16
N_GROUPS = D_MODEL // GROUP_CH
STATE = 64
CONV_WIDTH = 31
CONV_PAD = (CONV_WIDTH - 1) // 2
N_XHEADS = 4
XHEAD_DIM = D_MODEL // N_XHEADS
D_FF = 4 * D_MODEL
EPS = 1e-6

LANES = 128
N_SLABS = D_MODEL // LANES
GROUPS_PER_SLAB = LANES // GROUP_CH
S5_CHUNK = 16
N_HALF = S5_CHUNK // GROUPS_PER_SLAB
PAIR = 2
N_PAIRS = GROUPS_PER_SLAB // PAIR
PAIR_COLS = PAIR * S5_CHUNK * GROUP_CH
PAIR_STATE = 4 * PAIR * STATE
N_STATE_BLOCKS = 2 * N_PAIRS
STAGE_SLOTS = 4
STAGE_PITCH = S5_CHUNK + 8
HALO = 16
VMEM_LIMIT = 56 * 1024 * 1024

F32 = jnp.float32
BF16 = jnp.bfloat16


def _cparams(n_axes):
    return pltpu.CompilerParams(
        dimension_semantics=("arbitrary",) * n_axes, vmem_limit_bytes=VMEM_LIMIT)


def _const_spec(shape):
    nd = len(shape)
    return pl.BlockSpec(shape, lambda *_: (0,) * nd, pipeline_mode=pl.Buffered(1))


def _rms(x, g):
    return x * lax.rsqrt(jnp.mean(x * x, axis=-1, keepdims=True) + EPS) * g


def _token_tile(n, cap=512):
    t = min(cap, n)
    assert n % t == 0
    return t


BIG_TILE = 1024


def _row_parts(n):
    parts = 2 if n % 32 == 0 else 1
    return [slice(i * n // parts, (i + 1) * n // parts) for i in range(parts)]


def _kv_kernel(mem_ref, g_ref, w_ref, k_ref, v_ref):
    mn = _rms(mem_ref[0], g_ref[0]).astype(BF16)
    kv = jnp.dot(mn, w_ref[0], preferred_element_type=F32)
    k_ref[0, 0] = kv[:, :D_MODEL].astype(BF16)
    v_ref[0, 0] = kv[:, D_MODEL:].astype(BF16)


def _kv_all_layers(mem, norm_mem, w_kv):
    bsz, n_mem, _ = mem.shape
    depth = w_kv.shape[0]
    out = jax.ShapeDtypeStruct((depth, bsz, n_mem, D_MODEL), BF16)
    return pl.pallas_call(
        _kv_kernel,
        grid=(depth, bsz),
        in_specs=[
            pl.BlockSpec((1, n_mem, D_MODEL), lambda i, b: (b, 0, 0)),
            pl.BlockSpec((1, 1, D_MODEL), lambda i, b: (i, 0, 0)),
            pl.BlockSpec((1, D_MODEL, 2 * D_MODEL), lambda i, b: (i, 0, 0)),
        ],
        out_specs=[
            pl.BlockSpec((1, 1, n_mem, D_MODEL), lambda i, b: (i, b, 0, 0)),
            pl.BlockSpec((1, 1, n_mem, D_MODEL), lambda i, b: (i, b, 0, 0)),
        ],
        out_shape=[out, out],
        compiler_params=_cparams(2),
        name="kv_proj",
    )(mem, norm_mem.reshape(depth, 1, D_MODEL), w_kv)


def _store_rms_slabs(x, g, o_ref):
    xn = _rms(x, g).astype(BF16)
    for s in range(N_SLABS):
        o_ref[s] = xn[:, s * LANES:(s + 1) * LANES]


def _rms_slab_kernel(x_ref, g_ref, o_ref):
    _store_rms_slabs(x_ref[...], g_ref[...], o_ref)


def _rms_to_slabs(x2, g):
    n_tok = x2.shape[0]
    tt = _token_tile(n_tok)
    return pl.pallas_call(
        _rms_slab_kernel,
        grid=(n_tok // tt,),
        in_specs=[pl.BlockSpec((tt, D_MODEL), lambda t: (t, 0)), _const_spec((1, D_MODEL))],
        out_specs=pl.BlockSpec((N_SLABS, tt, LANES), lambda t: (0, t, 0)),
        out_shape=jax.ShapeDtypeStruct((N_SLABS, n_tok, LANES), BF16),
        compiler_params=_cparams(1),
        name="rms_slabs",
    )(x2, g.reshape(1, D_MODEL))


def _lane_block_transpose(vs):
    vs = list(vs)
    blk = lax.broadcasted_iota(jnp.int32, vs[0].shape, 1) // GROUP_CH
    d = 1
    while d < len(vs):
        upper = (blk & d) != 0
        for i in range(len(vs)):
            if i & d:
                continue
            a, b = vs[i], vs[i | d]
            vs[i] = jnp.where(upper, pltpu.roll(b, d * GROUP_CH, 1), a)
            vs[i | d] = jnp.where(upper, b, pltpu.roll(a, LANES - d * GROUP_CH, 1))
        d *= 2
    return vs


def _s5_kernel(u_ref, win_ref, wi_ref, wo_ref, lam_ref, d_ref, o_ref,
               ulhs_ref, g_ref, stage_ref, zs_ref, sf_ref, sb_ref, *, bb, nc, rt, rs, zp, pitch):
    tiles = [(b * nc + k0, b * pitch + k0) for b in range(bb) for k0 in range(0, nc, rt)]
    as_words = lambda v: pltpu.bitcast(v, jnp.int32)
    as_bf16 = lambda v: pltpu.bitcast(v, BF16)

    n_sub = 0
    for r0, p0 in tiles:
        for q0 in range(0, rt, rs):
            tok0 = (r0 + q0) * S5_CHUNK
            stage = stage_ref.at[n_sub % stage_ref.shape[0]]
            n_sub += 1
            tok = u_ref[0, tok0:tok0 + rs * S5_CHUNK, :].astype(F32)
            for k in range(rs):
                stage[k * STAGE_PITCH:k * STAGE_PITCH + S5_CHUNK, :] = tok[k * S5_CHUNK:(k + 1) * S5_CHUNK, :]
            steps = [as_words(stage[pl.ds(t, rs, stride=STAGE_PITCH), :].astype(BF16))
                     for t in range(S5_CHUNK)]
            runs = [_lane_block_transpose(steps[h * GROUPS_PER_SLAB:(h + 1) * GROUPS_PER_SLAB])
                    for h in range(N_HALF)]
            for pp in range(N_PAIRS):
                ulhs_ref[pp, r0 + q0:r0 + q0 + rs, :] = jnp.concatenate(
                    [as_bf16(runs[h][PAIR * pp + g]) for g in range(PAIR) for h in range(N_HALF)], axis=-1)
        for pp in range(N_PAIRS):
            s = jnp.dot(ulhs_ref[pp, r0:r0 + rt, :], win_ref[0, pp], preferred_element_type=F32)
            for c in range(2):
                sf_ref[2 * pp + c, p0:p0 + rt, :] = s[:, c * LANES:(c + 1) * LANES]
                sb_ref[2 * pp + c, p0:p0 + rt, :] = s[:, (2 + c) * LANES:(3 + c) * LANES]

    lam = lam_ref[0]

    def rows_of_chunk(k):
        return pl.ds(k, bb, stride=pitch) if bb > 1 else pl.ds(k, 1)

    def advance(s_ref, lam_row, idx, carry):
        new = [None] * N_STATE_BLOCKS
        for pp in range(N_PAIRS):
            re, im = 2 * pp, 2 * pp + 1
            l_re = lam_row[:, re * LANES:(re + 1) * LANES]
            l_im = lam_row[:, im * LANES:(im + 1) * LANES]
            s_re, s_im = s_ref[re, idx, :], s_ref[im, idx, :]
            c_re, c_im = carry[re], carry[im]
            s_ref[re, idx, :] = c_re
            s_ref[im, idx, :] = c_im
            new[re] = l_re * c_re - l_im * c_im + s_re
            new[im] = l_re * c_im + l_im * c_re + s_im
        return tuple(new)

    def body(k, carry):
        cf, cb = carry
        return (advance(sf_ref, lam[0:1], rows_of_chunk(k), cf),
                advance(sb_ref, lam[1:2], rows_of_chunk(nc - 1 - k), cb))

    zero = tuple(jnp.zeros((bb, LANES), F32) for _ in range(N_STATE_BLOCKS))
    lax.fori_loop(0, nc, body, (zero, zero))

    n_sub = 0
    for n_tile, (r0, p0) in enumerate(tiles):
        g_tile = g_ref.at[n_tile % g_ref.shape[0]]
        for pp in range(N_PAIRS):
            u = ulhs_ref[pp, r0:r0 + rt, :]
            entering = jnp.concatenate([s_ref[2 * pp + c, p0:p0 + rt, :]
                                        for s_ref in (sf_ref, sb_ref) for c in range(2)], axis=-1).astype(BF16)
            y = jnp.dot(u, wi_ref[0, pp], preferred_element_type=F32)
            y = y + jnp.dot(entering, wo_ref[0, pp], preferred_element_type=F32)
            g_tile[pp, 0:rt, :] = jax.nn.gelu(y + d_ref[0, pp] * u.astype(F32)).astype(BF16)
        for q0 in range(0, rt, rs):
            zs = zs_ref.at[n_sub % zs_ref.shape[0]]
            n_sub += 1
            for h in range(N_HALF):
                run = _lane_block_transpose([
                    as_words(g_tile[q // PAIR, q0:q0 + rs,
                                    ((q % PAIR) * N_HALF + h) * LANES:((q % PAIR) * N_HALF + h + 1) * LANES])
                    for q in range(GROUPS_PER_SLAB)])
                for i in range(GROUPS_PER_SLAB):
                    z0 = (h * GROUPS_PER_SLAB + i) * zp
                    zs[z0:z0 + rs, :] = as_bf16(run[i]).astype(F32)
            tok0 = (r0 + q0) * S5_CHUNK
            o_ref[0, tok0:tok0 + rs * S5_CHUNK, :] = jnp.concatenate(
                [zs[pl.ds(k, S5_CHUNK, stride=zp), :] for k in range(rs)], axis=0).astype(BF16)


def _s5_prepare(a_re, a_im, log_step, b_re, b_im, c_re, c_im, d_skip):
    t_len = S5_CHUNK
    step = jnp.exp(log_step.astype(F32))[..., None]
    ar, ai = a_re.astype(F32) * step, a_im.astype(F32) * step
    n = jnp.arange(t_len + 1, dtype=F32)[None, :, None, None]
    mag = jnp.exp(ar[:, None] * n)
    pw_re, pw_im = mag * jnp.cos(ai[:, None] * n), mag * jnp.sin(ai[:, None] * n)
    num_re, num_im = pw_re[:, 1] - 1.0, pw_im[:, 1]
    den = a_re * a_re + a_im * a_im
    z_re = (num_re * a_re + num_im * a_im) / den
    z_im = (num_im * a_re - num_re * a_im) / den
    bb_re = z_re[..., None] * b_re - z_im[..., None] * b_im
    bb_im = z_re[..., None] * b_im + z_im[..., None] * b_re

    t_idx = jnp.arange(t_len)
    n_pairs = N_GROUPS // PAIR
    tc = t_len * GROUP_CH
    cat = lambda *xs: jnp.concatenate(xs, axis=-1)

    def powers(d, idx):
        return tuple(v[d][idx].reshape(t_len, n_pairs, LANES).transpose(1, 0, 2) for v in (pw_re, pw_im))

    def b_rows(x):
        return x.reshape(n_pairs, PAIR, STATE, GROUP_CH).transpose(0, 3, 1, 2).reshape(n_pairs, GROUP_CH, LANES)

    def c_rows(x):
        return x.reshape(n_pairs, PAIR, GROUP_CH, STATE).transpose(0, 2, 1, 3).reshape(n_pairs, GROUP_CH, LANES)

    def pair_operator(p1, r1, p2, r2):
        x = p1[:, :, None, :] * r1[:, None, :, :] + p2[:, :, None, :] * r2[:, None, :, :]
        shape = (1, PAIR, 1, 1, PAIR_STATE)
        same = lax.broadcasted_iota(jnp.int32, shape, 1) == (lax.broadcasted_iota(jnp.int32, shape, 4) // STATE) % PAIR
        return jnp.where(same, x[:, None], 0.0).reshape(N_SLABS, N_PAIRS, PAIR_COLS, PAIR_STATE).astype(BF16)

    (pr_f, pi_f), (pr_b, pi_b) = powers(0, t_len - 1 - t_idx), powers(1, t_idx)
    br_f, bi_f, br_b, bi_b = b_rows(bb_re[0]), b_rows(bb_im[0]), b_rows(bb_re[1]), b_rows(bb_im[1])
    w_in = pair_operator(cat(pr_f, pr_f, pr_b, pr_b), cat(br_f, bi_f, br_b, bi_b),
                         cat(-pi_f, pi_f, -pi_b, pi_b), cat(bi_f, br_f, bi_b, br_b))

    (qr_f, qi_f), (qr_b, qi_b) = powers(0, t_idx + 1), powers(1, t_len - t_idx)
    cr_f, ci_f, cr_b, ci_b = c_rows(c_re[0]), c_rows(c_im[0]), c_rows(c_re[1]), c_rows(c_im[1])
    w_out_t = pair_operator(cat(qr_f, -qi_f, qr_b, -qi_b), cat(cr_f, cr_f, cr_b, cr_b),
                            cat(-qi_f, -qr_f, -qi_b, -qr_b), cat(ci_f, ci_f, ci_b, ci_b))

    def impulse(d, lags):
        pr, pi = (v[d][lags].transpose(1, 0, 2) for v in (pw_re, pw_im))
        z = (cat(pr, -pi)[:, :, None, :] * cat(c_re[d], c_re[d])[:, None, :, :]
             + cat(-pi, -pr)[:, :, None, :] * cat(c_im[d], c_im[d])[:, None, :, :])
        b = cat(bb_re[d].transpose(0, 2, 1), bb_im[d].transpose(0, 2, 1))
        return jnp.einsum('gcq,gmq->gcm', b, z.reshape(N_GROUPS, tc, 2 * STATE), precision=lax.Precision.HIGHEST)

    k_f, k_b = impulse(0, t_idx), impulse(1, t_len - 1 - t_idx)
    col = lax.broadcasted_iota(jnp.int32, (1, 1, tc), 2)
    toe = jnp.stack([
        jnp.where(col >= GROUP_CH * t, jnp.roll(k_f, GROUP_CH * t, axis=-1), 0.0)
        + jnp.where(col < GROUP_CH * (t + 1), jnp.roll(k_b, -GROUP_CH * (t_len - 1 - t), axis=-1), 0.0)
        for t in range(t_len)], axis=1)
    eye = jnp.eye(PAIR, dtype=F32)
    w_intra = (toe.reshape(n_pairs, PAIR, tc, 1, tc) * eye[None, :, None, :, None]).reshape(
        N_SLABS, N_PAIRS, PAIR_COLS, PAIR_COLS).astype(BF16)

    def pair_state(x_re, x_im):
        blocks = [v.reshape(N_SLABS, N_PAIRS, 1, PAIR * STATE) for v in (x_re, x_im)]
        return jnp.concatenate(blocks, axis=2).reshape(N_SLABS, N_STATE_BLOCKS * LANES)

    lam_t = jnp.stack([pair_state(pw_re[0, t_len], pw_im[0, t_len]),
                       pair_state(pw_re[1, t_len], pw_im[1, t_len])], axis=1)
    d = jnp.broadcast_to(d_skip.astype(F32).reshape(N_SLABS, N_PAIRS, PAIR, 1, GROUP_CH),
                         (N_SLABS, N_PAIRS, PAIR, t_len, GROUP_CH)).reshape(N_SLABS, N_PAIRS, 1, PAIR_COLS)
    return w_in, w_intra, jnp.swapaxes(w_out_t, 2, 3), lam_t, d


def _s5_scan(u_slabs, seq, prep):
    w_in, w_intra, w_out, lam_t, d = prep
    n_tok = u_slabs.shape[1]
    nc = seq // S5_CHUNK
    n_seq = n_tok // seq
    bb = max(1, min(n_seq, 1024 // nc))
    assert n_seq % bb == 0 and seq % S5_CHUNK == 0
    rows = bb * nc
    rt = min(128, nc)
    rs = min(64, rt)
    assert nc % rt == 0 and rt % rs == 0
    zp = rs + 8
    pitch = nc + 4 if nc % 8 == 0 else nc
    pair_w = pl.BlockSpec((1, N_PAIRS, PAIR_COLS, PAIR_COLS), lambda s, r: (s, 0, 0, 0))
    tok_spec = pl.BlockSpec((1, rows * S5_CHUNK, LANES), lambda s, r: (s, r, 0))
    state = pltpu.VMEM((N_STATE_BLOCKS, bb * pitch, LANES), F32)
    return pl.pallas_call(
        functools.partial(_s5_kernel, bb=bb, nc=nc, rt=rt, rs=rs, zp=zp, pitch=pitch),
        grid=(N_SLABS, n_seq // bb),
        in_specs=[
            tok_spec, pair_w, pair_w, pair_w,
            pl.BlockSpec((1, 2, N_STATE_BLOCKS * LANES), lambda s, r: (s, 0, 0)),
            pl.BlockSpec((1, N_PAIRS, 1, PAIR_COLS), lambda s, r: (s, 0, 0, 0)),
        ],
        out_specs=tok_spec,
        out_shape=jax.ShapeDtypeStruct(u_slabs.shape, BF16),
        scratch_shapes=[pltpu.VMEM((N_PAIRS, rows, PAIR_COLS), BF16),
                        pltpu.VMEM((2, N_PAIRS, rt, PAIR_COLS), BF16),
                        pltpu.VMEM((STAGE_SLOTS, rs * STAGE_PITCH, LANES), F32),
                        pltpu.VMEM((STAGE_SLOTS, S5_CHUNK * zp, LANES), F32), state, state],
        compiler_params=_cparams(2),
        name="s5_scan",
    )(u_slabs, w_in, w_intra, w_out, lam_t, d)


def _glu_kernel(g_ref, x_ref, w_ref, b_ref, o_ref):
    for rows in _row_parts(x_ref.shape[0]):
        g = jnp.concatenate([g_ref[s, rows, :] for s in range(N_SLABS)], axis=-1)
        h = jnp.dot(g, w_ref[...], preferred_element_type=F32) + b_ref[...]
        o_ref[rows, :] = x_ref[rows, :] + h[:, :D_MODEL] * jax.nn.sigmoid(h[:, D_MODEL:])


def _glu_residual(g_slabs, x2, w, b):
    n_tok = x2.shape[0]
    tt = _token_tile(n_tok, BIG_TILE)
    return pl.pallas_call(
        _glu_kernel,
        grid=(n_tok // tt,),
        in_specs=[
            pl.BlockSpec((N_SLABS, tt, LANES), lambda t: (0, t, 0)),
            pl.BlockSpec((tt, D_MODEL), lambda t: (t, 0)),
            _const_spec((D_MODEL, 2 * D_MODEL)),
            _const_spec((1, 2 * D_MODEL)),
        ],
        out_specs=pl.BlockSpec((tt, D_MODEL), lambda t: (t, 0)),
        out_shape=jax.ShapeDtypeStruct((n_tok, D_MODEL), F32),
        compiler_params=_cparams(1),
        name="s5_glu",
    )(g_slabs, x2, w, b.reshape(1, 2 * D_MODEL))


def _conv_in_kernel(x_ref, g_ref, w_ref, b_ref, o_ref):
    for rows in _row_parts(x_ref.shape[0]):
        xn = _rms(x_ref[rows, :], g_ref[...]).astype(BF16)
        h = jnp.dot(xn, w_ref[...], preferred_element_type=F32) + b_ref[...]
        o_ref[rows, :] = (h[:, :D_MODEL] * jax.nn.sigmoid(h[:, D_MODEL:])).astype(BF16)


def _conv_in(x2, g, w, b):
    n_tok = x2.shape[0]
    tt = _token_tile(n_tok, BIG_TILE)
    return pl.pallas_call(
        _conv_in_kernel,
        grid=(n_tok // tt,),
        in_specs=[
            pl.BlockSpec((tt, D_MODEL), lambda t: (t, 0)),
            _const_spec((1, D_MODEL)),
            _const_spec((D_MODEL, 2 * D_MODEL)),
            _const_spec((1, 2 * D_MODEL)),
        ],
        out_specs=pl.BlockSpec((tt, D_MODEL), lambda t: (t, 0)),
        out_shape=jax.ShapeDtypeStruct((n_tok, D_MODEL), BF16),
        compiler_params=_cparams(1),
        name="conv_in",
    )(x2, g.reshape(1, D_MODEL), w, b.reshape(1, 2 * D_MODEL))


def _conv_out_kernel(u_ref, up_ref, un_ref, x_ref, wdw_ref, bdw_ref, lg_ref, lb_ref, w_ref, b_ref,
                     o_ref, buf_ref, cv_ref, *, tt, rb):
    t = pl.program_id(1)
    last = pl.num_programs(1) - 1
    prev = jnp.where(t > 0, up_ref[0].astype(F32), 0.0)
    cur = u_ref[0].astype(F32)
    nxt = jnp.where(t < last, un_ref[0].astype(F32), 0.0)
    for s in range(N_SLABS):
        sl = slice(s * LANES, (s + 1) * LANES)
        buf_ref[s, 0:HALO, :] = prev[:, sl]
        buf_ref[s, HALO:HALO + tt, :] = cur[:, sl]
        buf_ref[s, HALO + tt:, :] = nxt[:, sl]
    off = HALO - CONV_PAD
    for s in range(N_SLABS):
        sl = slice(s * LANES, (s + 1) * LANES)
        for r0 in range(0, tt, rb):
            acc = jnp.broadcast_to(bdw_ref[:, sl], (rb, LANES))
            for k in range(CONV_WIDTH):
                acc = acc + wdw_ref[k:k + 1, sl] * buf_ref[s, r0 + off + k:r0 + off + k + rb, :]
            cv_ref[r0:r0 + rb, sl] = acc
    h = cv_ref[...]
    mu = jnp.mean(h, axis=-1, keepdims=True)
    hc = h - mu
    var = jnp.mean(hc * hc, axis=-1, keepdims=True)
    y = hc * lax.rsqrt(var + EPS) * lg_ref[...] + lb_ref[...]
    y = jax.nn.silu(y).astype(BF16)
    o_ref[0] = x_ref[0] + jnp.dot(y, w_ref[...], preferred_element_type=F32) + b_ref[...]


def _conv_out(u, x, w_dw, b_dw, ln_g, ln_b, w_out, b_out):
    bsz, seq, _ = x.shape
    tt = _token_tile(seq)
    assert tt % HALO == 0
    rb = min(128, tt)
    n_halo = seq // HALO
    per = tt // HALO
    row = lambda v: v.reshape(1, D_MODEL)
    return pl.pallas_call(
        functools.partial(_conv_out_kernel, tt=tt, rb=rb),
        grid=(bsz, seq // tt),
        in_specs=[
            pl.BlockSpec((1, tt, D_MODEL), lambda b, t: (b, t, 0)),
            pl.BlockSpec((1, HALO, D_MODEL), lambda b, t: (b, jnp.maximum(t * per - 1, 0), 0)),
            pl.BlockSpec((1, HALO, D_MODEL), lambda b, t: (b, jnp.minimum((t + 1) * per, n_halo - 1), 0)),
            pl.BlockSpec((1, tt, D_MODEL), lambda b, t: (b, t, 0)),
            _const_spec((CONV_WIDTH, D_MODEL)),
            _const_spec((1, D_MODEL)),
            _const_spec((1, D_MODEL)),
            _const_spec((1, D_MODEL)),
            _const_spec((D_MODEL, D_MODEL)),
            _const_spec((1, D_MODEL)),
        ],
        out_specs=pl.BlockSpec((1, tt, D_MODEL), lambda b, t: (b, t, 0)),
        out_shape=jax.ShapeDtypeStruct(x.shape, F32),
        scratch_shapes=[pltpu.VMEM((N_SLABS, tt + 2 * HALO, LANES), F32), pltpu.VMEM((tt, D_MODEL), F32)],
        compiler_params=_cparams(2),
        name="conv_out",
    )(u, u, u, x, w_dw, row(b_dw), row(ln_g), row(ln_b), w_out, row(b_out))


def _attn_kernel(x_ref, k_ref, v_ref, g_ref, wq_ref, wo_ref, o_ref):
    x = x_ref[0]
    xn = _rms(x, g_ref[...]).astype(BF16)
    q = jnp.dot(xn, wq_ref[...], preferred_element_type=F32)
    q = (q * (XHEAD_DIM ** -0.5)).astype(BF16)
    heads = []
    for h in range(N_XHEADS):
        sl = slice(h * XHEAD_DIM, (h + 1) * XHEAD_DIM)
        s = lax.dot_general(q[:, sl], k_ref[0, 0, :, sl], (((1,), (1,)), ((), ())),
                            preferred_element_type=F32)
        e = jnp.exp(s - jnp.max(s, axis=-1, keepdims=True))
        o_h = jnp.dot(e.astype(BF16), v_ref[0, 0, :, sl], preferred_element_type=F32)
        heads.append((o_h / jnp.sum(e, axis=-1, keepdims=True)).astype(BF16))
    o = jnp.concatenate(heads, axis=-1)
    o_ref[0] = x + jnp.dot(o, wo_ref[...], preferred_element_type=F32)


def _cross_attention(x, k_all, v_all, layer, g, wq, wo):
    bsz, seq, _ = x.shape
    n_mem = k_all.shape[2]
    tt = _token_tile(seq, BIG_TILE)
    kv_spec = pl.BlockSpec((1, 1, n_mem, D_MODEL), lambda b, t: (layer, b, 0, 0))
    return pl.pallas_call(
        _attn_kernel,
        grid=(bsz, seq // tt),
        in_specs=[
            pl.BlockSpec((1, tt, D_MODEL), lambda b, t: (b, t, 0)),
            kv_spec, kv_spec,
            _const_spec((1, D_MODEL)),
            _const_spec((D_MODEL, D_MODEL)),
            _const_spec((D_MODEL, D_MODEL)),
        ],
        out_specs=pl.BlockSpec((1, tt, D_MODEL), lambda b, t: (b, t, 0)),
        out_shape=jax.ShapeDtypeStruct(x.shape, F32),
        compiler_params=_cparams(2),
        name="cross_attn",
    )(x, k_all, v_all, g.reshape(1, D_MODEL), wq, wo)


def _mlp_kernel(x_ref, g_ref, win_ref, wout_ref, gn_ref, o_ref, *rest, mode):
    x = x_ref[...]
    xn = _rms(x, g_ref[...]).astype(BF16)
    acc = x
    for c in range(D_FF // D_MODEL):
        sl = slice(c * D_MODEL, (c + 1) * D_MODEL)
        h = jnp.dot(xn, win_ref[:, sl], preferred_element_type=F32)
        h = jnp.square(jnp.maximum(h, 0.0)).astype(BF16)
        acc = acc + jnp.dot(h, wout_ref[sl, :], preferred_element_type=F32)
    if mode == "final":
        o_ref[...] = _rms(acc, gn_ref[...])
        return
    o_ref[...] = acc
    if mode == "s5_next":
        _store_rms_slabs(acc, gn_ref[...], rest[0])


def _mlp(x2, g, w_in, w_out, g_next, mode):
    n_tok = x2.shape[0]
    tt = _token_tile(n_tok, BIG_TILE)
    out_specs = [pl.BlockSpec((tt, D_MODEL), lambda t: (t, 0))]
    out_shape = [jax.ShapeDtypeStruct((n_tok, D_MODEL), F32)]
    if mode == "s5_next":
        out_specs.append(pl.BlockSpec((N_SLABS, tt, LANES), lambda t: (0, t, 0)))
        out_shape.append(jax.ShapeDtypeStruct((N_SLABS, n_tok, LANES), BF16))
    return pl.pallas_call(
        functools.partial(_mlp_kernel, mode=mode),
        grid=(n_tok // tt,),
        in_specs=[
            pl.BlockSpec((tt, D_MODEL), lambda t: (t, 0)),
            _const_spec((1, D_MODEL)),
            _const_spec((D_MODEL, D_FF)),
            _const_spec((D_FF, D_MODEL)),
            _const_spec((1, D_MODEL)),
        ],
        out_specs=out_specs,
        out_shape=out_shape,
        compiler_params=_cparams(1),
        name="mlp_" + mode,
    )(x2, g.reshape(1, D_MODEL), w_in, w_out, g_next.reshape(1, D_MODEL))


def _trunk(x, mem, p, s5_preps):
    bsz, seq, _ = x.shape
    depth = p['attn_w_q'].shape[0]
    k_all, v_all = _kv_all_layers(mem, p['norm_mem'], p['attn_w_kv'])
    flat = lambda a: a.reshape(bsz * seq, D_MODEL)
    u = None
    for i in range(depth):
        j = i // 2
        if i % 2 == 0:
            if u is None:
                u = _rms_to_slabs(flat(x), p['norm_mix'][i])
            g = _s5_scan(u, seq, s5_preps[j])
            x = _glu_residual(g, flat(x), p['ssm_w_glu'][j], p['ssm_b_glu'][j]).reshape(x.shape)
        else:
            u = _conv_in(flat(x), p['norm_mix'][i], p['conv_w_in'][j], p['conv_b_in'][j])
            x = _conv_out(u.reshape(x.shape), x, p['conv_w_dw'][j], p['conv_b_dw'][j],
                          p['conv_ln_g'][j], p['conv_ln_b'][j], p['conv_w_out'][j], p['conv_b_out'][j])
        x = _cross_attention(x, k_all, v_all, i, p['norm_xq'][i], p['attn_w_q'][i], p['attn_w_o'][i])
        if i == depth - 1:
            mode, g_next = "final", p['norm_final']
        elif (i + 1) % 2 == 0:
            mode, g_next = "s5_next", p['norm_mix'][i + 1]
        else:
            mode, g_next = "plain", p['norm_final']
        out = _mlp(flat(x), p['norm_ffn'][i], p['mlp_w_in'][i], p['mlp_w_out'][i], g_next, mode)
        x = out[0].reshape(x.shape)
        u = out[1] if mode == "s5_next" else None
    return x


def kernel(x_prompt, x_sample, mem_prompt, mem_sample, norm_mix, norm_xq, norm_mem, norm_ffn, norm_final, ssm_a_re, ssm_a_im, ssm_log_step, ssm_b_re, ssm_b_im, ssm_c_re, ssm_c_im, ssm_d, ssm_w_glu, ssm_b_glu, conv_w_in, conv_b_in, conv_w_dw, conv_b_dw, conv_ln_g, conv_ln_b, conv_w_out, conv_b_out, attn_w_q, attn_w_kv, attn_w_o, mlp_w_in, mlp_w_out):
    p = dict(norm_mix=norm_mix, norm_xq=norm_xq, norm_mem=norm_mem, norm_ffn=norm_ffn,
             norm_final=norm_final, ssm_b_glu=ssm_b_glu,
             conv_b_in=conv_b_in, conv_w_dw=conv_w_dw, conv_b_dw=conv_b_dw,
             conv_ln_g=conv_ln_g, conv_ln_b=conv_ln_b, conv_b_out=conv_b_out)
    for name, w in (('ssm_w_glu', ssm_w_glu), ('conv_w_in', conv_w_in), ('conv_w_out', conv_w_out),
                    ('attn_w_q', attn_w_q), ('attn_w_kv', attn_w_kv), ('attn_w_o', attn_w_o),
                    ('mlp_w_in', mlp_w_in), ('mlp_w_out', mlp_w_out)):
        p[name] = w.astype(BF16)
    stacked = jax.vmap(_s5_prepare)(ssm_a_re, ssm_a_im, ssm_log_step, ssm_b_re, ssm_b_im,
                                    ssm_c_re, ssm_c_im, ssm_d)
    s5_preps = [tuple(a[j] for a in stacked) for j in range(ssm_a_re.shape[0])]
    y_prompt = _trunk(x_prompt, mem_prompt, p, s5_preps)
    y_sample = _trunk(x_sample, mem_sample, p, s5_preps)
    return (y_prompt, y_sample)
```

```python
import functools
import math

import jax
import jax.numpy as jnp
from jax import lax
from jax.experimental import pallas as pl
from jax.experimental.pallas import tpu as pltpu

D_MODEL = 1024
GROUP_CH = 16
N_GROUPS = D_MODEL // GROUP_CH
STATE = 64
CONV_WIDTH = 31
CONV_PAD = (CONV_WIDTH - 1) // 2
N_XHEADS = 4
XHEAD_DIM = D_MODEL // N_XHEADS
D_FF = 4 * D_MODEL
EPS = 1e-6

LANES = 128
N_SLABS = D_MODEL // LANES
GROUPS_PER_SLAB = LANES // GROUP_CH
S5_CHUNK = 16
N_HALF = S5_CHUNK // GROUPS_PER_SLAB
PAIR = 2
N_PAIRS = GROUPS_PER_SLAB // PAIR
PAIR_COLS = PAIR * S5_CHUNK * GROUP_CH
PAIR_STATE = 4 * PAIR * STATE
N_STATE_BLOCKS = 2 * N_PAIRS
STAGE_SLOTS = 4
STAGE_PITCH = S5_CHUNK + 8
HALO = 16
VMEM_LIMIT = 56 * 1024 * 1024

F32 = jnp.float32
BF16 = jnp.bfloat16


def _cparams(n_axes):
    return pltpu.CompilerParams(
        dimension_semantics=("arbitrary",) * n_axes, vmem_limit_bytes=VMEM_LIMIT)


def _const_spec(shape):
    nd = len(shape)
    return pl.BlockSpec(shape, lambda *_: (0,) * nd, pipeline_mode=pl.Buffered(1))


def _rms(x, g):
    return x * lax.rsqrt(jnp.mean(x * x, axis=-1, keepdims=True) + EPS) * g


def _token_tile(n, cap=512):
    t = min(cap, n)
    assert n % t == 0
    return t


BIG_TILE = 1024


def _row_parts(n):
    parts = 4 if n % 1024 == 0 else 2 if n % 32 == 0 else 1
    return [slice(i * n // parts, (i + 1) * n // parts) for i in range(parts)]


def _kv_kernel(mem_ref, g_ref, w_ref, k_ref, v_ref):
    mn = _rms(mem_ref[0], g_ref[0]).astype(BF16)
    kv = jnp.dot(mn, w_ref[0], preferred_element_type=F32)
    k_ref[0, 0] = kv[:, :D_MODEL].astype(BF16)
    v_ref[0, 0] = kv[:, D_MODEL:].astype(BF16)


def _kv_all_layers(mem, norm_mem, w_kv):
    bsz, n_mem, _ = mem.shape
    depth = w_kv.shape[0]
    out = jax.ShapeDtypeStruct((depth, bsz, n_mem, D_MODEL), BF16)
    return pl.pallas_call(
        _kv_kernel,
        grid=(depth, bsz),
        in_specs=[
            pl.BlockSpec((1, n_mem, D_MODEL), lambda i, b: (b, 0, 0)),
            pl.BlockSpec((1, 1, D_MODEL), lambda i, b: (i, 0, 0)),
            pl.BlockSpec((1, D_MODEL, 2 * D_MODEL), lambda i, b: (i, 0, 0)),
        ],
        out_specs=[
            pl.BlockSpec((1, 1, n_mem, D_MODEL), lambda i, b: (i, b, 0, 0)),
            pl.BlockSpec((1, 1, n_mem, D_MODEL), lambda i, b: (i, b, 0, 0)),
        ],
        out_shape=[out, out],
        compiler_params=_cparams(2),
        name="kv_proj",
    )(mem, norm_mem.reshape(depth, 1, D_MODEL), w_kv)


def _store_rms_slabs(x, g, o_ref):
    xn = _rms(x, g).astype(BF16)
    for s in range(N_SLABS):
        o_ref[s] = xn[:, s * LANES:(s + 1) * LANES]


def _rms_slab_kernel(x_ref, g_ref, o_ref):
    _store_rms_slabs(x_ref[...], g_ref[...], o_ref)


def _rms_to_slabs(x2, g):
    n_tok = x2.shape[0]
    tt = _token_tile(n_tok)
    return pl.pallas_call(
        _rms_slab_kernel,
        grid=(n_tok // tt,),
        in_specs=[pl.BlockSpec((tt, D_MODEL), lambda t: (t, 0)), _const_spec((1, D_MODEL))],
        out_specs=pl.BlockSpec((N_SLABS, tt, LANES), lambda t: (0, t, 0)),
        out_shape=jax.ShapeDtypeStruct((N_SLABS, n_tok, LANES), BF16),
        compiler_params=_cparams(1),
        name="rms_slabs",
    )(x2, g.reshape(1, D_MODEL))


def _lane_block_transpose(vs):
    vs = list(vs)
    blk = lax.broadcasted_iota(jnp.int32, vs[0].shape, 1) // GROUP_CH
    d = 1
    while d < len(vs):
        upper = (blk & d) != 0
        for i in range(len(vs)):
            if i & d:
                continue
            a, b = vs[i], vs[i | d]
            vs[i] = jnp.where(upper, pltpu.roll(b, d * GROUP_CH, 1), a)
            vs[i | d] = jnp.where(upper, b, pltpu.roll(a, LANES - d * GROUP_CH, 1))
        d *= 2
    return vs


def _s5_kernel(u_ref, win_ref, wi_ref, wo_ref, lam_ref, d_ref, o_ref,
               ulhs_ref, g_ref, stage_ref, zs_ref, sf_ref, sb_ref, *, bb, nc, rt, rs, zp, pitch):
    tiles = [(b * nc + k0, b * pitch + k0) for b in range(bb) for k0 in range(0, nc, rt)]
    as_words = lambda v: pltpu.bitcast(v, jnp.int32)
    as_bf16 = lambda v: pltpu.bitcast(v, BF16)

    n_sub = 0
    for r0, p0 in tiles:
        for q0 in range(0, rt, rs):
            tok0 = (r0 + q0) * S5_CHUNK
            stage = stage_ref.at[n_sub % stage_ref.shape[0]]
            n_sub += 1
            tok = u_ref[0, tok0:tok0 + rs * S5_CHUNK, :].astype(F32)
            for k in range(rs):
                stage[k * STAGE_PITCH:k * STAGE_PITCH + S5_CHUNK, :] = tok[k * S5_CHUNK:(k + 1) * S5_CHUNK, :]
            steps = [as_words(stage[pl.ds(t, rs, stride=STAGE_PITCH), :].astype(BF16))
                     for t in range(S5_CHUNK)]
            runs = [_lane_block_transpose(steps[h * GROUPS_PER_SLAB:(h + 1) * GROUPS_PER_SLAB])
                    for h in range(N_HALF)]
            for pp in range(N_PAIRS):
                ulhs_ref[pp, r0 + q0:r0 + q0 + rs, :] = jnp.concatenate(
                    [as_bf16(runs[h][PAIR * pp + g]) for g in range(PAIR) for h in range(N_HALF)], axis=-1)
        for pp in range(N_PAIRS):
            s = jnp.dot(ulhs_ref[pp, r0:r0 + rt, :], win_ref[0, pp], preferred_element_type=F32)
            for c in range(2):
                sf_ref[2 * pp + c, p0:p0 + rt, :] = s[:, c * LANES:(c + 1) * LANES]
                sb_ref[2 * pp + c, p0:p0 + rt, :] = s[:, (2 + c) * LANES:(3 + c) * LANES]

    lam = lam_ref[0]

    def rows_of_chunk(k):
        return pl.ds(k, bb, stride=pitch) if bb > 1 else pl.ds(k, 1)

    def advance(s_ref, lam_row, idx, carry):
        new = [None] * N_STATE_BLOCKS
        for pp in range(N_PAIRS):
            re, im = 2 * pp, 2 * pp + 1
            l_re = lam_row[:, re * LANES:(re + 1) * LANES]
            l_im = lam_row[:, im * LANES:(im + 1) * LANES]
            s_re, s_im = s_ref[re, idx, :], s_ref[im, idx, :]
            c_re, c_im = carry[re], carry[im]
            s_ref[re, idx, :] = c_re
            s_ref[im, idx, :] = c_im
            new[re] = l_re * c_re - l_im * c_im + s_re
            new[im] = l_re * c_im + l_im * c_re + s_im
        return tuple(new)

    def body(k, carry):
        cf, cb = carry
        return (advance(sf_ref, lam[0:1], rows_of_chunk(k), cf),
                advance(sb_ref, lam[1:2], rows_of_chunk(nc - 1 - k), cb))

    zero = tuple(jnp.zeros((bb, LANES), F32) for _ in range(N_STATE_BLOCKS))
    lax.fori_loop(0, nc, body, (zero, zero))

    n_sub = 0
    for n_tile, (r0, p0) in enumerate(tiles):
        g_tile = g_ref.at[n_tile % g_ref.shape[0]]
        for pp in range(N_PAIRS):
            u = ulhs_ref[pp, r0:r0 + rt, :]
            entering = jnp.concatenate([s_ref[2 * pp + c, p0:p0 + rt, :]
                                        for s_ref in (sf_ref, sb_ref) for c in range(2)], axis=-1).astype(BF16)
            y = jnp.dot(u, wi_ref[0, pp], preferred_element_type=F32)
            y = y + jnp.dot(entering, wo_ref[0, pp], preferred_element_type=F32)
            g_tile[pp, 0:rt, :] = jax.nn.gelu(y + d_ref[0, pp] * u.astype(F32)).astype(BF16)
        for q0 in range(0, rt, rs):
            zs = zs_ref.at[n_sub % zs_ref.shape[0]]
            n_sub += 1
            for h in range(N_HALF):
                run = _lane_block_transpose([
                    as_words(g_tile[q // PAIR, q0:q0 + rs,
                                    ((q % PAIR) * N_HALF + h) * LANES:((q % PAIR) * N_HALF + h + 1) * LANES])
                    for q in range(GROUPS_PER_SLAB)])
                for i in range(GROUPS_PER_SLAB):
                    z0 = (h * GROUPS_PER_SLAB + i) * zp
                    zs[z0:z0 + rs, :] = as_bf16(run[i]).astype(F32)
            tok0 = (r0 + q0) * S5_CHUNK
            o_ref[0, tok0:tok0 + rs * S5_CHUNK, :] = jnp.concatenate(
                [zs[pl.ds(k, S5_CHUNK, stride=zp), :] for k in range(rs)], axis=0).astype(BF16)


def _s5_prepare(a_re, a_im, log_step, b_re, b_im, c_re, c_im, d_skip):
    t_len = S5_CHUNK
    step = jnp.exp(log_step.astype(F32))[..., None]
    ar, ai = a_re.astype(F32) * step, a_im.astype(F32) * step
    n = jnp.arange(t_len + 1, dtype=F32)[None, :, None, None]
    mag = jnp.exp(ar[:, None] * n)
    pw_re, pw_im = mag * jnp.cos(ai[:, None] * n), mag * jnp.sin(ai[:, None] * n)
    num_re, num_im = pw_re[:, 1] - 1.0, pw_im[:, 1]
    den = a_re * a_re + a_im * a_im
    z_re = (num_re * a_re + num_im * a_im) / den
    z_im = (num_im * a_re - num_re * a_im) / den
    bb_re = z_re[..., None] * b_re - z_im[..., None] * b_im
    bb_im = z_re[..., None] * b_im + z_im[..., None] * b_re

    t_idx = jnp.arange(t_len)
    n_pairs = N_GROUPS // PAIR
    tc = t_len * GROUP_CH
    cat = lambda *xs: jnp.concatenate(xs, axis=-1)

    def powers(d, idx):
        return tuple(v[d][idx].reshape(t_len, n_pairs, LANES).transpose(1, 0, 2) for v in (pw_re, pw_im))

    def b_rows(x):
        return x.reshape(n_pairs, PAIR, STATE, GROUP_CH).transpose(0, 3, 1, 2).reshape(n_pairs, GROUP_CH, LANES)

    def c_rows(x):
        return x.reshape(n_pairs, PAIR, GROUP_CH, STATE).transpose(0, 2, 1, 3).reshape(n_pairs, GROUP_CH, LANES)

    def pair_operator(p1, r1, p2, r2):
        x = p1[:, :, None, :] * r1[:, None, :, :] + p2[:, :, None, :] * r2[:, None, :, :]
        shape = (1, PAIR, 1, 1, PAIR_STATE)
        same = lax.broadcasted_iota(jnp.int32, shape, 1) == (lax.broadcasted_iota(jnp.int32, shape, 4) // STATE) % PAIR
        return jnp.where(same, x[:, None], 0.0).reshape(N_SLABS, N_PAIRS, PAIR_COLS, PAIR_STATE).astype(BF16)

    (pr_f, pi_f), (pr_b, pi_b) = powers(0, t_len - 1 - t_idx), powers(1, t_idx)
    br_f, bi_f, br_b, bi_b = b_rows(bb_re[0]), b_rows(bb_im[0]), b_rows(bb_re[1]), b_rows(bb_im[1])
    w_in = pair_operator(cat(pr_f, pr_f, pr_b, pr_b), cat(br_f, bi_f, br_b, bi_b),
                         cat(-pi_f, pi_f, -pi_b, pi_b), cat(bi_f, br_f, bi_b, br_b))

    (qr_f, qi_f), (qr_b, qi_b) = powers(0, t_idx + 1), powers(1, t_len - t_idx)
    cr_f, ci_f, cr_b, ci_b = c_rows(c_re[0]), c_rows(c_im[0]), c_rows(c_re[1]), c_rows(c_im[1])
    w_out_t = pair_operator(cat(qr_f, -qi_f, qr_b, -qi_b), cat(cr_f, cr_f, cr_b, cr_b),
                            cat(-qi_f, -qr_f, -qi_b, -qr_b), cat(ci_f, ci_f, ci_b, ci_b))

    def impulse(d, lags):
        pr, pi = (v[d][lags].transpose(1, 0, 2) for v in (pw_re, pw_im))
        z = (cat(pr, -pi)[:, :, None, :] * cat(c_re[d], c_re[d])[:, None, :, :]
             + cat(-pi, -pr)[:, :, None, :] * cat(c_im[d], c_im[d])[:, None, :, :])
        b = cat(bb_re[d].transpose(0, 2, 1), bb_im[d].transpose(0, 2, 1))
        return jnp.einsum('gcq,gmq->gcm', b, z.reshape(N_GROUPS, tc, 2 * STATE), precision=lax.Precision.HIGHEST)

    k_f, k_b = impulse(0, t_idx), impulse(1, t_len - 1 - t_idx)
    col = lax.broadcasted_iota(jnp.int32, (1, 1, tc), 2)
    toe = jnp.stack([
        jnp.where(col >= GROUP_CH * t, jnp.roll(k_f, GROUP_CH * t, axis=-1), 0.0)
        + jnp.where(col < GROUP_CH * (t + 1), jnp.roll(k_b, -GROUP_CH * (t_len - 1 - t), axis=-1), 0.0)
        for t in range(t_len)], axis=1)
    eye = jnp.eye(PAIR, dtype=F32)
    w_intra = (toe.reshape(n_pairs, PAIR, tc, 1, tc) * eye[None, :, None, :, None]).reshape(
        N_SLABS, N_PAIRS, PAIR_COLS, PAIR_COLS).astype(BF16)

    def pair_state(x_re, x_im):
        blocks = [v.reshape(N_SLABS, N_PAIRS, 1, PAIR * STATE) for v in (x_re, x_im)]
        return jnp.concatenate(blocks, axis=2).reshape(N_SLABS, N_STATE_BLOCKS * LANES)

    lam_t = jnp.stack([pair_state(pw_re[0, t_len], pw_im[0, t_len]),
                       pair_state(pw_re[1, t_len], pw_im[1, t_len])], axis=1)
    d = jnp.broadcast_to(d_skip.astype(F32).reshape(N_SLABS, N_PAIRS, PAIR, 1, GROUP_CH),
                         (N_SLABS, N_PAIRS, PAIR, t_len, GROUP_CH)).reshape(N_SLABS, N_PAIRS, 1, PAIR_COLS)
    return w_in, w_intra, jnp.swapaxes(w_out_t, 2, 3), lam_t, d


def _s5_scan(u_slabs, seq, prep):
    w_in, w_intra, w_out, lam_t, d = prep
    n_tok = u_slabs.shape[1]
    nc = seq // S5_CHUNK
    n_seq = n_tok // seq
    bb = max(1, min(n_seq, 1024 // nc))
    assert n_seq % bb == 0 and seq % S5_CHUNK == 0
    rows = bb * nc
    rt = min(128, nc)
    rs = min(64, rt)
    assert nc % rt == 0 and rt % rs == 0
    zp = rs + 8
    pitch = nc + 4 if nc % 8 == 0 else nc
    pair_w = pl.BlockSpec((1, N_PAIRS, PAIR_COLS, PAIR_COLS), lambda s, r: (s, 0, 0, 0))
    tok_spec = pl.BlockSpec((1, rows * S5_CHUNK, LANES), lambda s, r: (s, r, 0))
    state = pltpu.VMEM((N_STATE_BLOCKS, bb * pitch, LANES), F32)
    return pl.pallas_call(
        functools.partial(_s5_kernel, bb=bb, nc=nc, rt=rt, rs=rs, zp=zp, pitch=pitch),
        grid=(N_SLABS, n_seq // bb),
        in_specs=[
            tok_spec, pair_w, pair_w, pair_w,
            pl.BlockSpec((1, 2, N_STATE_BLOCKS * LANES), lambda s, r: (s, 0, 0)),
            pl.BlockSpec((1, N_PAIRS, 1, PAIR_COLS), lambda s, r: (s, 0, 0, 0)),
        ],
        out_specs=tok_spec,
        out_shape=jax.ShapeDtypeStruct(u_slabs.shape, BF16),
        scratch_shapes=[pltpu.VMEM((N_PAIRS, rows, PAIR_COLS), BF16),
                        pltpu.VMEM((2, N_PAIRS, rt, PAIR_COLS), BF16),
                        pltpu.VMEM((STAGE_SLOTS, rs * STAGE_PITCH, LANES), F32),
                        pltpu.VMEM((STAGE_SLOTS, S5_CHUNK * zp, LANES), F32), state, state],
        compiler_params=_cparams(2),
        name="s5_scan",
    )(u_slabs, w_in, w_intra, w_out, lam_t, d)


def _glu_kernel(g_ref, x_ref, w_ref, b_ref, o_ref):
    for rows in _row_parts(x_ref.shape[0]):
        g = jnp.concatenate([g_ref[s, rows, :] for s in range(N_SLABS)], axis=-1)
        h = jnp.dot(g, w_ref[...], preferred_element_type=F32) + b_ref[...]
        o_ref[rows, :] = x_ref[rows, :] + h[:, :D_MODEL] * jax.nn.sigmoid(h[:, D_MODEL:])


def _glu_residual(g_slabs, x2, w, b):
    n_tok = x2.shape[0]
    tt = _token_tile(n_tok, BIG_TILE)
    return pl.pallas_call(
        _glu_kernel,
        grid=(n_tok // tt,),
        in_specs=[
            pl.BlockSpec((N_SLABS, tt, LANES), lambda t: (0, t, 0)),
            pl.BlockSpec((tt, D_MODEL), lambda t: (t, 0)),
            _const_spec((D_MODEL, 2 * D_MODEL)),
            _const_spec((1, 2 * D_MODEL)),
        ],
        out_specs=pl.BlockSpec((tt, D_MODEL), lambda t: (t, 0)),
        out_shape=jax.ShapeDtypeStruct((n_tok, D_MODEL), F32),
        compiler_params=_cparams(1),
        name="s5_glu",
    )(g_slabs, x2, w, b.reshape(1, 2 * D_MODEL))


def _conv_in_kernel(x_ref, g_ref, w_ref, b_ref, o_ref):
    for rows in _row_parts(x_ref.shape[0]):
        xn = _rms(x_ref[rows, :], g_ref[...]).astype(BF16)
        h = jnp.dot(xn, w_ref[...], preferred_element_type=F32) + b_ref[...]
        o_ref[rows, :] = (h[:, :D_MODEL] * jax.nn.sigmoid(h[:, D_MODEL:])).astype(BF16)


def _conv_in(x2, g, w, b):
    n_tok = x2.shape[0]
    tt = _token_tile(n_tok, BIG_TILE)
    return pl.pallas_call(
        _conv_in_kernel,
        grid=(n_tok // tt,),
        in_specs=[
            pl.BlockSpec((tt, D_MODEL), lambda t: (t, 0)),
            _const_spec((1, D_MODEL)),
            _const_spec((D_MODEL, 2 * D_MODEL)),
            _const_spec((1, 2 * D_MODEL)),
        ],
        out_specs=pl.BlockSpec((tt, D_MODEL), lambda t: (t, 0)),
        out_shape=jax.ShapeDtypeStruct((n_tok, D_MODEL), BF16),
        compiler_params=_cparams(1),
        name="conv_in",
    )(x2, g.reshape(1, D_MODEL), w, b.reshape(1, 2 * D_MODEL))


def _conv_out_kernel(u_ref, up_ref, un_ref, x_ref, wdw_ref, bdw_ref, lg_ref, lb_ref, w_ref, b_ref,
                     o_ref, buf_ref, cv_ref, *, tt, rb):
    t = pl.program_id(1)
    last = pl.num_programs(1) - 1
    prev = jnp.where(t > 0, up_ref[0].astype(F32), 0.0)
    cur = u_ref[0].astype(F32)
    nxt = jnp.where(t < last, un_ref[0].astype(F32), 0.0)
    for s in range(N_SLABS):
        sl = slice(s * LANES, (s + 1) * LANES)
        buf_ref[s, 0:HALO, :] = prev[:, sl]
        buf_ref[s, HALO:HALO + tt, :] = cur[:, sl]
        buf_ref[s, HALO + tt:, :] = nxt[:, sl]
    off = HALO - CONV_PAD
    for s in range(N_SLABS):
        sl = slice(s * LANES, (s + 1) * LANES)
        for r0 in range(0, tt, rb):
            acc = jnp.broadcast_to(bdw_ref[:, sl], (rb, LANES))
            for k in range(CONV_WIDTH):
                acc = acc + wdw_ref[k:k + 1, sl] * buf_ref[s, r0 + off + k:r0 + off + k + rb, :]
            cv_ref[r0:r0 + rb, sl] = acc
    h = cv_ref[...]
    mu = jnp.mean(h, axis=-1, keepdims=True)
    hc = h - mu
    var = jnp.mean(hc * hc, axis=-1, keepdims=True)
    y = hc * lax.rsqrt(var + EPS) * lg_ref[...] + lb_ref[...]
    y = jax.nn.silu(y).astype(BF16)
    o_ref[0] = x_ref[0] + jnp.dot(y, w_ref[...], preferred_element_type=F32) + b_ref[...]


def _conv_out(u, x, w_dw, b_dw, ln_g, ln_b, w_out, b_out):
    bsz, seq, _ = x.shape
    tt = _token_tile(seq)
    assert tt % HALO == 0
    rb = min(128, tt)
    n_halo = seq // HALO
    per = tt // HALO
    row = lambda v: v.reshape(1, D_MODEL)
    return pl.pallas_call(
        functools.partial(_conv_out_kernel, tt=tt, rb=rb),
        grid=(bsz, seq // tt),
        in_specs=[
            pl.BlockSpec((1, tt, D_MODEL), lambda b, t: (b, t, 0)),
            pl.BlockSpec((1, HALO, D_MODEL), lambda b, t: (b, jnp.maximum(t * per - 1, 0), 0)),
            pl.BlockSpec((1, HALO, D_MODEL), lambda b, t: (b, jnp.minimum((t + 1) * per, n_halo - 1), 0)),
            pl.BlockSpec((1, tt, D_MODEL), lambda b, t: (b, t, 0)),
            _const_spec((CONV_WIDTH, D_MODEL)),
            _const_spec((1, D_MODEL)),
            _const_spec((1, D_MODEL)),
            _const_spec((1, D_MODEL)),
            _const_spec((D_MODEL, D_MODEL)),
            _const_spec((1, D_MODEL)),
        ],
        out_specs=pl.BlockSpec((1, tt, D_MODEL), lambda b, t: (b, t, 0)),
        out_shape=jax.ShapeDtypeStruct(x.shape, F32),
        scratch_shapes=[pltpu.VMEM((N_SLABS, tt + 2 * HALO, LANES), F32), pltpu.VMEM((tt, D_MODEL), F32)],
        compiler_params=_cparams(2),
        name="conv_out",
    )(u, u, u, x, w_dw, row(b_dw), row(ln_g), row(ln_b), w_out, row(b_out))


def _attn_kernel(x_ref, k_ref, v_ref, g_ref, wq_ref, wo_ref, o_ref):
    x = x_ref[0]
    xn = _rms(x, g_ref[...]).astype(BF16)
    q = jnp.dot(xn, wq_ref[...], preferred_element_type=F32)
    q = (q * (XHEAD_DIM ** -0.5)).astype(BF16)
    heads = []
    for h in range(N_XHEADS):
        sl = slice(h * XHEAD_DIM, (h + 1) * XHEAD_DIM)
        s = lax.dot_general(q[:, sl], k_ref[0, 0, :, sl], (((1,), (1,)), ((), ())),
                            preferred_element_type=F32)
        e = jnp.exp(s - jnp.max(s, axis=-1, keepdims=True))
        o_h = jnp.dot(e.astype(BF16), v_ref[0, 0, :, sl], preferred_element_type=F32)
        heads.append((o_h / jnp.sum(e, axis=-1, keepdims=True)).astype(BF16))
    o = jnp.concatenate(heads, axis=-1)
    o_ref[0] = x + jnp.dot(o, wo_ref[...], preferred_element_type=F32)


def _cross_attention(x, k_all, v_all, layer, g, wq, wo):
    bsz, seq, _ = x.shape
    n_mem = k_all.shape[2]
    tt = _token_tile(seq, BIG_TILE)
    kv_spec = pl.BlockSpec((1, 1, n_mem, D_MODEL), lambda b, t: (layer, b, 0, 0))
    return pl.pallas_call(
        _attn_kernel,
        grid=(bsz, seq // tt),
        in_specs=[
            pl.BlockSpec((1, tt, D_MODEL), lambda b, t: (b, t, 0)),
            kv_spec, kv_spec,
            _const_spec((1, D_MODEL)),
            _const_spec((D_MODEL, D_MODEL)),
            _const_spec((D_MODEL, D_MODEL)),
        ],
        out_specs=pl.BlockSpec((1, tt, D_MODEL), lambda b, t: (b, t, 0)),
        out_shape=jax.ShapeDtypeStruct(x.shape, F32),
        compiler_params=_cparams(2),
        name="cross_attn",
    )(x, k_all, v_all, g.reshape(1, D_MODEL), wq, wo)


def _mlp_kernel(x_ref, g_ref, win_ref, wout_ref, gn_ref, o_ref, *rest, mode):
    x = x_ref[...]
    xn = _rms(x, g_ref[...]).astype(BF16)
    acc = x
    for c in range(D_FF // D_MODEL):
        sl = slice(c * D_MODEL, (c + 1) * D_MODEL)
        h = jnp.dot(xn, win_ref[:, sl], preferred_element_type=F32)
        h = jnp.square(jnp.maximum(h, 0.0)).astype(BF16)
        acc = acc + jnp.dot(h, wout_ref[sl, :], preferred_element_type=F32)
    if mode == "final":
        o_ref[...] = _rms(acc, gn_ref[...])
        return
    o_ref[...] = acc
    if mode == "s5_next":
        _store_rms_slabs(acc, gn_ref[...], rest[0])


def _mlp(x2, g, w_in, w_out, g_next, mode):
    n_tok = x2.shape[0]
    tt = _token_tile(n_tok, BIG_TILE)
    out_specs = [pl.BlockSpec((tt, D_MODEL), lambda t: (t, 0))]
    out_shape = [jax.ShapeDtypeStruct((n_tok, D_MODEL), F32)]
    if mode == "s5_next":
        out_specs.append(pl.BlockSpec((N_SLABS, tt, LANES), lambda t: (0, t, 0)))
        out_shape.append(jax.ShapeDtypeStruct((N_SLABS, n_tok, LANES), BF16))
    return pl.pallas_call(
        functools.partial(_mlp_kernel, mode=mode),
        grid=(n_tok // tt,),
        in_specs=[
            pl.BlockSpec((tt, D_MODEL), lambda t: (t, 0)),
            _const_spec((1, D_MODEL)),
            _const_spec((D_MODEL, D_FF)),
            _const_spec((D_FF, D_MODEL)),
            _const_spec((1, D_MODEL)),
        ],
        out_specs=out_specs,
        out_shape=out_shape,
        compiler_params=_cparams(1),
        name="mlp_" + mode,
    )(x2, g.reshape(1, D_MODEL), w_in, w_out, g_next.reshape(1, D_MODEL))


def _trunk(x, mem, p, s5_preps):
    bsz, seq, _ = x.shape
    depth = p['attn_w_q'].shape[0]
    k_all, v_all = _kv_all_layers(mem, p['norm_mem'], p['attn_w_kv'])
    flat = lambda a: a.reshape(bsz * seq, D_MODEL)
    u = None
    for i in range(depth):
        j = i // 2
        if i % 2 == 0:
            if u is None:
                u = _rms_to_slabs(flat(x), p['norm_mix'][i])
            g = _s5_scan(u, seq, s5_preps[j])
            x = _glu_residual(g, flat(x), p['ssm_w_glu'][j], p['ssm_b_glu'][j]).reshape(x.shape)
        else:
            u = _conv_in(flat(x), p['norm_mix'][i], p['conv_w_in'][j], p['conv_b_in'][j])
            x = _conv_out(u.reshape(x.shape), x, p['conv_w_dw'][j], p['conv_b_dw'][j],
                          p['conv_ln_g'][j], p['conv_ln_b'][j], p['conv_w_out'][j], p['conv_b_out'][j])
        x = _cross_attention(x, k_all, v_all, i, p['norm_xq'][i], p['attn_w_q'][i], p['attn_w_o'][i])
        if i == depth - 1:
            mode, g_next = "final", p['norm_final']
        elif (i + 1) % 2 == 0:
            mode, g_next = "s5_next", p['norm_mix'][i + 1]
        else:
            mode, g_next = "plain", p['norm_final']
        out = _mlp(flat(x), p['norm_ffn'][i], p['mlp_w_in'][i], p['mlp_w_out'][i], g_next, mode)
        x = out[0].reshape(x.shape)
        u = out[1] if mode == "s5_next" else None
    return x


def kernel(x_prompt, x_sample, mem_prompt, mem_sample, norm_mix, norm_xq, norm_mem, norm_ffn, norm_final, ssm_a_re, ssm_a_im, ssm_log_step, ssm_b_re, ssm_b_im, ssm_c_re, ssm_c_im, ssm_d, ssm_w_glu, ssm_b_glu, conv_w_in, conv_b_in, conv_w_dw, conv_b_dw, conv_ln_g, conv_ln_b, conv_w_out, conv_b_out, attn_w_q, attn_w_kv, attn_w_o, mlp_w_in, mlp_w_out):
    p = dict(norm_mix=norm_mix, norm_xq=norm_xq, norm_mem=norm_mem, norm_ffn=norm_ffn,
             norm_final=norm_final, ssm_b_glu=ssm_b_glu,
             conv_b_in=conv_b_in, conv_w_dw=conv_w_dw, conv_b_dw=conv_b_dw,
             conv_ln_g=conv_ln_g, conv_ln_b=conv_ln_b, conv_b_out=conv_b_out)
    for name, w in (('ssm_w_glu', ssm_w_glu), ('conv_w_in', conv_w_in), ('conv_w_out', conv_w_out),
                    ('attn_w_q', attn_w_q), ('attn_w_kv', attn_w_kv), ('attn_w_o', attn_w_o),
                    ('mlp_w_in', mlp_w_in), ('mlp_w_out', mlp_w_out)):
        p[name] = w.astype(BF16)
    stacked = jax.vmap(_s5_prepare)(ssm_a_re, ssm_a_im, ssm_log_step, ssm_b_re, ssm_b_im,
                                    ssm_c_re, ssm_c_im, ssm_d)
    s5_preps = [tuple(a[j] for a in stacked) for j in range(ssm_a_re.shape[0])]
    y_prompt = _trunk(x_prompt, mem_prompt, p, s5_preps)
    y_sample = _trunk(x_sample, mem_sample, p, s5_preps)
    return (y_prompt, y_sample)
```
